```python
import math
import jax, jax.numpy as jnp
from jax import lax
import numpy as np

D_MODEL = 2048
BATCH = 2
SEQ = 16384
DEPTH = 2

N_META = 16
EPS = 1e-6
NEG_BIG = -1e30
F_FLOOR = 1e-30
HG_HEADS = 8
HG_DIM = 128
HG_WIDTH = HG_HEADS * HG_DIM
HG_CHUNK = 64
RW_HEAD = 64
RW_HEADS = 16
RW_WIDTH = RW_HEADS * RW_HEAD
RW_DECAY_LORA = 64
RW_AAA_LORA = 64
RW_GATE_LORA = 160
RW_GN_EPS = 64e-5
MLA_HEADS = 8
MLA_Q_RANK = 512
MLA_KV_RANK = 512
MLA_NOPE = 128
MLA_ROPE = 64
MLA_V = 128
MLA_WIDTH = MLA_HEADS * MLA_V
ROPE_BASE = 10000.0
Q_BLOCK = 128
D_FF = 5632
CONV_W = 3
N_BRANCH = 3

HG_COLS = 4 * HG_WIDTH
RW_COLS = 3 * RW_WIDTH + RW_DECAY_LORA + RW_AAA_LORA + RW_GATE_LORA
MLA_COLS = MLA_Q_RANK + MLA_KV_RANK + MLA_ROPE
GATE_COLS = N_BRANCH * D_MODEL
IN_COLS = HG_COLS + RW_COLS + MLA_COLS + GATE_COLS

kernel_name = "hybrid_hgrn2_rwkv7_mla_convffn"


def split_cols(z, sizes):
    idx = np.cumsum(sizes)[:-1].tolist()
    return jnp.split(z, idx, axis=-1)


def rmsnorm(x, g, eps=EPS):
    xf = x.astype(jnp.float32)
    y = xf * lax.rsqrt(jnp.mean(xf * xf, axis=-1, keepdims=True) + eps)
    return (y * g).astype(x.dtype)


def shift1(z):
    return jnp.concatenate([jnp.zeros_like(z[:, :1]), z[:, :-1]], axis=1)


def _gla_chunk(S, q, k, v, logf):
    L = q.shape[1]
    b = jnp.cumsum(logf, axis=1)
    causal = jnp.tril(jnp.ones((L, L), dtype=bool))[None, :, :, None, None]
    diff = b[:, :, None] - b[:, None, :]
    decay = jnp.where(causal, jnp.exp(jnp.where(causal, diff, 0.0)), 0.0)
    A = jnp.einsum('bthd,bshd,btshd->bhts', q, k, decay)
    o = jnp.einsum('bhts,bshv->bthv', A, v) + jnp.einsum('bthd,bhdv->bthv', q * jnp.exp(b), S)
    bL = b[:, -1]
    S_new = S * jnp.exp(bL)[..., None] + jnp.einsum('bshd,bshv->bhdv', k * jnp.exp(bL[:, None] - b), v)
    return S_new, o


def hgrn2(zq, zf, zi, zg, lb, norm_g):
    B, T, _ = zq.shape
    f32 = jnp.float32
    heads = lambda a: a.astype(f32).reshape(B, T, HG_HEADS, HG_DIM)
    zf = zf.astype(f32)
    lb = lb.astype(f32)
    f = lb + (1.0 - lb) * jax.nn.sigmoid(zf)
    logf = jnp.log(jnp.maximum(f, F_FLOOR))
    k = (1.0 - lb) * jax.nn.sigmoid(-zf)
    q, k, i, logf = map(heads, (zq, k, zi, logf))
    S0 = jnp.zeros((B, HG_HEADS, HG_DIM, HG_DIM), f32)
    S, o_meta = _gla_chunk(S0, q[:, :N_META], k[:, :N_META], i[:, :N_META], logf[:, :N_META])
    nc = (T - N_META) // HG_CHUNK
    chunks = lambda a: jnp.moveaxis(a[:, N_META:].reshape(B, nc, HG_CHUNK, HG_HEADS, HG_DIM), 1, 0)
    _, o_real = lax.scan(lambda s, c: _gla_chunk(s, *c), S, tuple(map(chunks, (q, k, i, logf))))
    o_real = jnp.moveaxis(o_real, 0, 1).reshape(B, T - N_META, HG_HEADS, HG_DIM)
    o = jnp.concatenate([o_meta, o_real], axis=1)
    o = o * lax.rsqrt(jnp.mean(o * o, axis=-1, keepdims=True) + EPS) * norm_g.astype(f32).reshape(HG_HEADS, HG_DIM)
    return (o.reshape(B, T, HG_WIDTH) * jax.nn.silu(zg.astype(f32))).astype(zq.dtype)


def rwkv7(zr, zk, zv, zw, za, zg, w0, w2, a0, a2, g2, k_k, k_a, r_k, ln_g, ln_b):
    B, T, _ = zr.shape
    f32 = jnp.float32
    out_dtype = zr.dtype
    zr, zk, zv, zw, za, zg = (t.astype(f32) for t in (zr, zk, zv, zw, za, zg))
    w0, w2, a0, a2, g2, k_k, k_a = (t.astype(f32) for t in (w0, w2, a0, a2, g2, k_k, k_a))
    heads = lambda a: a.reshape(B, T, RW_HEADS, RW_HEAD)
    w = -jax.nn.softplus(-(w0 + jnp.tanh(zw) @ w2)) - 0.5
    decay = jnp.exp(-jnp.exp(w))
    a = jax.nn.sigmoid(a0 + za @ a2)
    g = jax.nn.sigmoid(zg) @ g2
    kk = heads(zk * k_k)
    kk = kk / jnp.maximum(jnp.sqrt(jnp.sum(kk * kk, axis=-1, keepdims=True) + 1e-24), 1e-12)
    k = zk * (1.0 + (a - 1.0) * k_a)
    r, k, v, a, decay = map(heads, (zr, k, zv, a, decay))

    def step(S, inp):
        r_t, w_t, k_t, v_t, kk_t, a_t = inp
        sa = jnp.einsum('bhvk,bhk->bhv', S, kk_t)
        S = (S * w_t[:, :, None, :] - sa[..., None] * (kk_t * a_t)[:, :, None, :]
             + v_t[..., None] * k_t[:, :, None, :])
        return S, jnp.einsum('bhvk,bhk->bhv', S, r_t)

    tm = lambda t: jnp.moveaxis(t, 1, 0)
    S0 = jnp.zeros((B, RW_HEADS, RW_HEAD, RW_HEAD), f32)
    _, y = lax.scan(step, S0, tuple(map(tm, (r, decay, k, v, kk, a))))
    y = jnp.moveaxis(y, 0, 1)
    mu = jnp.mean(y, axis=-1, keepdims=True)
    var = jnp.mean(jnp.square(y - mu), axis=-1, keepdims=True)
    y = ((y - mu) * lax.rsqrt(var + RW_GN_EPS) * ln_g.astype(f32).reshape(RW_HEADS, RW_HEAD)
         + ln_b.astype(f32).reshape(RW_HEADS, RW_HEAD))
    y = y + jnp.sum(r * k * r_k.astype(f32), axis=-1, keepdims=True) * v
    return (y.reshape(B, T, RW_WIDTH) * g).astype(out_dtype)


def rope_tables(T):
    inv = ROPE_BASE ** (-jnp.arange(0, MLA_ROPE, 2, dtype=jnp.float32) / MLA_ROPE)
    ang = jnp.arange(T, dtype=jnp.float32)[:, None] * inv[None, :]
    return jnp.cos(ang), jnp.sin(ang)


def apply_rope(x, cos, sin):
    x1, x2 = jnp.split(x.astype(jnp.float32), 2, axis=-1)
    return jnp.concatenate([x1 * cos - x2 * sin, x2 * cos + x1 * sin], axis=-1).astype(x.dtype)


def mla(cq, ckv, kr, q_norm_g, w_uq, kv_norm_g, w_ukv, cos, sin):
    B, T, _ = cq.shape
    q = (rmsnorm(cq, q_norm_g) @ w_uq).reshape(B, T, MLA_HEADS, MLA_NOPE + MLA_ROPE)
    kv = (rmsnorm(ckv, kv_norm_g) @ w_ukv).reshape(B, T, MLA_HEADS, MLA_NOPE + MLA_V)
    q_nope = q[..., :MLA_NOPE]
    q_rope = apply_rope(q[..., MLA_NOPE:], cos[:, None, :], sin[:, None, :])
    k_nope, v = kv[..., :MLA_NOPE], kv[..., MLA_NOPE:]
    k_rope = apply_rope(kr, cos, sin)
    scale = (MLA_NOPE + MLA_ROPE) ** -0.5
    kpos = jnp.arange(T)

    def attend(q_n, q_r, qpos):
        s = jnp.einsum('bqhd,bkhd->bhqk', q_n, k_nope) + jnp.einsum('bqhr,bkr->bhqk', q_r, k_rope)
        s = jnp.where(kpos[None, :] <= qpos[:, None], s.astype(jnp.float32) * scale, NEG_BIG)
        p = jax.nn.softmax(s, axis=-1).astype(v.dtype)
        return jnp.einsum('bhqk,bkhv->bqhv', p, v)

    o_meta = attend(q_nope[:, :N_META], q_rope[:, :N_META], jnp.arange(N_META))
    nb = (T - N_META) // Q_BLOCK
    blocks = lambda a: jnp.moveaxis(a[:, N_META:].reshape(B, nb, Q_BLOCK, *a.shape[2:]), 1, 0)
    qpos = (N_META + jnp.arange(T - N_META)).reshape(nb, Q_BLOCK)
    o_real = lax.map(lambda c: attend(*c), (blocks(q_nope), blocks(q_rope), qpos))
    o_real = jnp.moveaxis(o_real, 0, 1).reshape(B, T - N_META, MLA_HEADS, MLA_V)
    return jnp.concatenate([o_meta, o_real], axis=1).reshape(B, T, MLA_WIDTH)


def conv_ffn(hn, w_up, conv_w, conv_b, w_down):
    u = hn @ w_up
    uc = lax.conv_general_dilated(u, conv_w[:, None, :].astype(u.dtype), window_strides=(1,),
                                  padding=[(CONV_W - 1, 0)],
                                  dimension_numbers=('NWC', 'WIO', 'NWC'),
                                  feature_group_count=u.shape[-1]) + conv_b
    gate, val = jnp.split(uc, 2, axis=-1)
    return (jax.nn.silu(gate) * val) @ w_down


def setup_inputs(seed: int = 0) -> dict:
    key = jax.random.key(seed)
    ks = iter(jax.random.split(key, 40))
    nrm = lambda shape, scale: scale * jax.random.normal(next(ks), shape, jnp.float32)
    L = DEPTH
    return {
        "x": nrm((BATCH, SEQ, D_MODEL), 1.0),
        "meta_tokens": nrm((N_META, D_MODEL), 1.0),
        "norm1_g": 1.0 + nrm((L, D_MODEL), 0.02),
        "w_in": nrm((L, D_MODEL, IN_COLS), D_MODEL ** -0.5),
        "hg_lb_logits": nrm((L, HG_WIDTH), 0.5),
        "hg_norm_g": 1.0 + nrm((L, HG_WIDTH), 0.02),
        "rw_mu": jax.random.uniform(next(ks), (L, RW_COLS), jnp.float32),
        "rw_w0": nrm((L, RW_WIDTH), 0.5),
        "rw_w2": nrm((L, RW_DECAY_LORA, RW_WIDTH), RW_DECAY_LORA ** -0.5),
        "rw_a0": nrm((L, RW_WIDTH), 0.1),
        "rw_a2": nrm((L, RW_AAA_LORA, RW_WIDTH), RW_AAA_LORA ** -0.5),
        "rw_g2": nrm((L, RW_GATE_LORA, RW_WIDTH), RW_GATE_LORA ** -0.5),
        "rw_k_k": 0.85 + nrm((L, RW_WIDTH), 0.05),
        "rw_k_a": 1.0 + nrm((L, RW_WIDTH), 0.05),
        "rw_r_k": nrm((L, RW_HEADS, RW_HEAD), 0.1),
        "rw_ln_g": 1.0 + nrm((L, RW_WIDTH), 0.02),
        "rw_ln_b": nrm((L, RW_WIDTH), 0.02),
        "mla_q_norm_g": 1.0 + nrm((L, MLA_Q_RANK), 0.02),
        "mla_w_uq": nrm((L, MLA_Q_RANK, MLA_HEADS * (MLA_NOPE + MLA_ROPE)), MLA_Q_RANK ** -0.5),
        "mla_kv_norm_g": 1.0 + nrm((L, MLA_KV_RANK), 0.02),
        "mla_w_ukv": nrm((L, MLA_KV_RANK, MLA_HEADS * (MLA_NOPE + MLA_V)), MLA_KV_RANK ** -0.5),
        "w_branch": nrm((L, N_BRANCH, HG_WIDTH, D_MODEL), HG_WIDTH ** -0.5),
        "w_out": nrm((L, D_MODEL, D_MODEL), D_MODEL ** -0.5),
        "norm2_g": 1.0 + nrm((L, D_MODEL), 0.02),
        "ffn_w_up": nrm((L, D_MODEL, 2 * D_FF), D_MODEL ** -0.5),
        "ffn_conv_w": nrm((L, CONV_W, 2 * D_FF), CONV_W ** -0.5),
        "ffn_conv_b": nrm((L, 2 * D_FF), 0.02),
        "ffn_w_down": nrm((L, D_FF, D_MODEL), D_FF ** -0.5),
        "final_norm_g": 1.0 + nrm((D_MODEL,), 0.02),
    }


def reference(x, meta_tokens, norm1_g, w_in, hg_lb_logits, hg_norm_g, rw_mu, rw_w0, rw_w2,
              rw_a0, rw_a2, rw_g2, rw_k_k, rw_k_a, rw_r_k, rw_ln_g, rw_ln_b, mla_q_norm_g,
              mla_w_uq, mla_kv_norm_g, mla_w_ukv, w_branch, w_out, norm2_g, ffn_w_up,
              ffn_conv_w, ffn_conv_b, ffn_w_down, final_norm_g):
    B = x.shape[0]
    meta = jnp.broadcast_to(meta_tokens[None].astype(x.dtype), (B, N_META, D_MODEL))
    h = jnp.concatenate([meta, x], axis=1)
    T = h.shape[1]
    cos, sin = rope_tables(T)
    p_lb = jax.nn.softmax(hg_lb_logits.astype(jnp.float32), axis=0)
    lower_bounds = jnp.cumsum(p_lb, axis=0) - p_lb[0]
    for l in range(DEPTH):
        hn = rmsnorm(h, norm1_g[l])
        z = hn @ w_in[l]
        hq, hf, hi, hgate, rw_z, cq, ckv, kr, gates = split_cols(
            z, [HG_WIDTH] * 4 + [RW_COLS, MLA_Q_RANK, MLA_KV_RANK, MLA_ROPE, GATE_COLS])
        rw_z = rw_z + rw_mu[l] * (shift1(rw_z) - rw_z)
        zr, zk, zv, zw, za, zg = split_cols(
            rw_z, [RW_WIDTH] * 3 + [RW_DECAY_LORA, RW_AAA_LORA, RW_GATE_LORA])
        y_a = hgrn2(hq, hf, hi, hgate, lower_bounds[l], hg_norm_g[l])
        y_b = rwkv7(zr, zk, zv, zw, za, zg, rw_w0[l], rw_w2[l], rw_a0[l], rw_a2[l], rw_g2[l],
                    rw_k_k[l], rw_k_a[l], rw_r_k[l], rw_ln_g[l], rw_ln_b[l])
        y_c = mla(cq, ckv, kr, mla_q_norm_g[l], mla_w_uq[l], mla_kv_norm_g[l], mla_w_ukv[l], cos, sin)
        ys = jnp.stack([y_a.astype(h.dtype), y_b.astype(h.dtype), y_c.astype(h.dtype)], axis=2)
        g = jax.nn.sigmoid(gates).reshape(B, T, N_BRANCH, D_MODEL)
        merged = jnp.einsum('btnd,btnd->btd', g, jnp.einsum('btnw,nwd->btnd', ys, w_branch[l]))
        h = h + merged @ w_out[l]
        h = h + conv_ffn(rmsnorm(h, norm2_g[l]), ffn_w_up[l], ffn_conv_w[l], ffn_conv_b[l], ffn_w_down[l])
    return rmsnorm(h[:, N_META:], final_norm_g)
```

```python
import functools

import numpy as np
import jax
import jax.numpy as jnp
from jax import lax
from jax.experimental import pallas as pl
from jax.experimental.pallas import tpu as pltpu

F32 = jnp.float32
BF16 = jnp.bfloat16
HIGHEST = lax.Precision.HIGHEST

N_META = 16
EPS = 1e-6
NEG_BIG = -1e30
F_FLOOR = 1e-30
HG_HEADS = 8
HG_DIM = 128
HG_WIDTH = HG_HEADS * HG_DIM
RW_HEAD = 64
RW_HEADS = 16
RW_WIDTH = RW_HEADS * RW_HEAD
RW_DECAY_LORA = 64
RW_AAA_LORA = 64
RW_GATE_LORA = 160
RW_GN_EPS = 64e-5
MLA_HEADS = 8
MLA_Q_RANK = 512
MLA_KV_RANK = 512
MLA_NOPE = 128
MLA_ROPE = 64
MLA_V = 128
MLA_WIDTH = MLA_HEADS * MLA_V
ROPE_BASE = 10000.0
CONV_W = 3
N_BRANCH = 3

LANES = 128
SUBLANES = 8
VMEM_LIMIT = 56 * 1024 * 1024

Z_HQ, Z_HF, Z_HI, Z_HG = 0, 1024, 2048, 3072
Z_R, Z_K, Z_V = 4096, 5120, 6144
Z_CQ, Z_CKV = 7168, 7680
Z_GL = 8192
Z_WA = 8448
Z_KR = 8576
Z_GATES = 8704
Z_COLS = Z_GATES + N_BRANCH * 2048

RW_CHUNK = 64
HG_CHUNK = 64
HG_BASE = 16


def _params(sem):
    return pltpu.CompilerParams(dimension_semantics=sem, vmem_limit_bytes=VMEM_LIMIT)


def _pick(n, cands):
    for c in cands:
        if n % c == 0:
            return c
    raise ValueError(f"no tile for {n} in {cands}")


def _block_id(idx, size):
    return lax.shift_right_logical(idx, int(np.log2(size)))


def _softplus(x):
    return jnp.maximum(x, 0.0) + jnp.log(1.0 + jnp.exp(-jnp.abs(x)))


def _row_valid(row, tp, pad, nb):
    valid = None
    for b in range(nb):
        ok = jnp.logical_or(row < b * tp, row >= b * tp + pad)
        valid = ok if valid is None else jnp.logical_and(valid, ok)
    return valid


def _normed_rows(x, g, eps, row0, tp, pad, nb):
    ms = jnp.mean(x * x, axis=-1, keepdims=True)
    y = x * lax.rsqrt(ms + eps) * g
    if pad:
        row = row0 + lax.broadcasted_iota(jnp.int32, (x.shape[0], 1), 0)
        y = jnp.where(_row_valid(row, tp, pad, nb), y, 0.0)
    return y


def _norm_matmul_kernel(x_ref, g_ref, w_ref, o_ref, xn_ref, *, tm, tp, pad, nb, eps):
    i = pl.program_id(0)

    @pl.when(pl.program_id(1) == 0)
    def _():
        xn_ref[...] = _normed_rows(x_ref[...], g_ref[...], eps, i * tm, tp, pad, nb).astype(BF16)

    o_ref[...] = jnp.dot(xn_ref[...], w_ref[...], preferred_element_type=F32).astype(o_ref.dtype)


def _norm_matmul(x, col_block, k, g, w, out_dtype, *, tp, pad, nb, tm, tn):
    m = x.shape[0]
    n = w.shape[1]
    kern = functools.partial(_norm_matmul_kernel, tm=tm, tp=tp, pad=pad, nb=nb, eps=EPS)
    return pl.pallas_call(
        kern,
        grid=(m // tm, n // tn),
        in_specs=[pl.BlockSpec((tm, k), lambda i, j: (i, col_block)),
                  pl.BlockSpec((1, k), lambda i, j: (0, 0)),
                  pl.BlockSpec((k, tn), lambda i, j: (0, j))],
        out_specs=pl.BlockSpec((tm, tn), lambda i, j: (i, j)),
        out_shape=jax.ShapeDtypeStruct((m, n), out_dtype),
        scratch_shapes=[pltpu.VMEM((tm, k), BF16)],
        compiler_params=_params(("arbitrary", "arbitrary")),
        name="norm_matmul",
    )(x, g.reshape(1, k).astype(F32), w)


def _hgrn_diag(q, k, v, b, base):
    n = q.shape[0]
    r = lax.broadcasted_iota(jnp.int32, (n, 1), 0) & (base - 1)
    o = jnp.zeros_like(q)
    for lag in range(base):
        if lag == 0:
            ks, bs, vs = k, b, v
        else:
            ks = pltpu.roll(k, lag, 0)
            bs = pltpu.roll(b, lag, 0)
            vs = pltpu.roll(v, lag, 0)
        ok = r >= lag
        d = jnp.where(ok, b - bs, 0.0)
        a = jnp.sum(q * ks * jnp.exp(d), axis=-1, keepdims=True)
        o = o + jnp.where(ok, a, 0.0) * vs
    return o


def _hgrn_cross(q, k, v, b, lo, hi, base):
    n = hi - lo
    if n == base:
        return None
    mid = lo + n // 2
    b_ref = b[mid - 1:mid, :]
    qg = q[mid:hi] * jnp.exp(b[mid:hi] - b_ref)
    kg = k[lo:mid] * jnp.exp(b_ref - b[lo:mid])
    a = lax.dot_general(qg, kg, (((1,), (1,)), ((), ())), precision=HIGHEST, preferred_element_type=F32)
    o_hi = jnp.dot(a, v[lo:mid], precision=HIGHEST, preferred_element_type=F32)
    o_lo_in = _hgrn_cross(q, k, v, b, lo, mid, base)
    o_hi_in = _hgrn_cross(q, k, v, b, mid, hi, base)
    if o_hi_in is not None:
        o_hi = o_hi + o_hi_in
    if o_lo_in is None:
        o_lo_in = jnp.zeros((mid - lo, q.shape[1]), F32)
    return jnp.concatenate([o_lo_in, o_hi], axis=0)


def _hgrn_kernel(q_ref, f_ref, i_ref, g_ref, lb_ref, ng_ref, o_ref, st_ref, *, rows, chunk, pad):
    c = pl.program_id(2)

    @pl.when(c == 0)
    def _():
        st_ref[...] = jnp.zeros_like(st_ref)

    lb = lb_ref[...]
    ng = ng_ref[...]
    tri = (lax.broadcasted_iota(jnp.int32, (chunk, chunk), 0)
           >= lax.broadcasted_iota(jnp.int32, (chunk, chunk), 1)).astype(F32)
    for ci in range(rows // chunk):
        sl = pl.ds(ci * chunk, chunk)
        q = q_ref[sl, :]
        zf = f_ref[sl, :]
        v = i_ref[sl, :]
        zg = g_ref[sl, :]
        pos = c * rows + ci * chunk + lax.broadcasted_iota(jnp.int32, (chunk, 1), 0)
        valid = pos >= pad
        f = lb + (1.0 - lb) * jax.nn.sigmoid(zf)
        logf = jnp.where(valid, jnp.log(jnp.maximum(f, F_FLOOR)), 0.0)
        k = jnp.where(valid, (1.0 - lb) * jax.nn.sigmoid(-zf), 0.0)
        b = jnp.dot(tri, logf, precision=HIGHEST, preferred_element_type=F32)
        st = st_ref[...]
        o = _hgrn_diag(q, k, v, b, HG_BASE)
        o_cross = _hgrn_cross(q, k, v, b, 0, chunk, HG_BASE)
        if o_cross is not None:
            o = o + o_cross
        o = o + lax.dot_general(q * jnp.exp(b), st, (((1,), (1,)), ((), ())),
                                precision=HIGHEST, preferred_element_type=F32)
        b_last = b[chunk - 1:chunk, :]
        kg = k * jnp.exp(b_last - b)
        st_ref[...] = st * jnp.exp(b_last) + jnp.dot(v.T, kg, precision=HIGHEST, preferred_element_type=F32)
        o = o * lax.rsqrt(jnp.mean(o * o, axis=-1, keepdims=True) + EPS) * ng
        o_ref[sl, :] = (o * (zg * jax.nn.sigmoid(zg))).astype(o_ref.dtype)


def _hgrn2(z, lb, norm_g, *, nb, tp, pad):
    m = z.shape[0]
    rows = _pick(tp, (256, 128, 64))
    nrb = tp // rows
    kern = functools.partial(_hgrn_kernel, rows=rows, chunk=HG_CHUNK, pad=pad)

    def zspec(off):
        return pl.BlockSpec((rows, HG_DIM), lambda b, h, c: (b * nrb + c, off // HG_DIM + h))

    pspec = pl.BlockSpec((1, HG_DIM), lambda b, h, c: (0, h))
    return pl.pallas_call(
        kern,
        grid=(nb, HG_HEADS, nrb),
        in_specs=[zspec(Z_HQ), zspec(Z_HF), zspec(Z_HI), zspec(Z_HG), pspec, pspec],
        out_specs=pl.BlockSpec((rows, HG_DIM), lambda b, h, c: (b * nrb + c, h)),
        out_shape=jax.ShapeDtypeStruct((m, HG_WIDTH), BF16),
        scratch_shapes=[pltpu.VMEM((HG_DIM, HG_DIM), F32)],
        compiler_params=_params(("arbitrary", "arbitrary", "arbitrary")),
        name="hgrn2",
    )(z, z, z, z, lb.reshape(1, HG_WIDTH).astype(F32), norm_g.reshape(1, HG_WIDTH).astype(F32))


def _rwkv_pre_kernel(zr_ref, zk_ref, zv_ref, zg_ref, zwa_ref,
                     mr_ref, mk_ref, mv_ref, mg_ref, mwa_ref,
                     w0_ref, w2_ref, a0_ref, a2_ref, g2_ref, kk_ref, ka_ref,
                     r_out, lw_out, k_out, v_out, kn_out, a_out, g_out,
                     er_ref, ek_ref, ev_ref, eg_ref, ewa_ref, *, tm, tp, pad, nb):
    i = pl.program_id(0)

    def shifted(z_ref, e_ref, mu_ref):
        @pl.when(i == 0)
        def _():
            e_ref[0:SUBLANES, :] = jnp.zeros((SUBLANES, e_ref.shape[1]), F32)

        x = z_ref[...]
        e_ref[SUBLANES:SUBLANES + tm, :] = x
        prev = e_ref[SUBLANES - 1:SUBLANES - 1 + tm, :]
        e_ref[0:SUBLANES, :] = x[tm - SUBLANES:tm, :]
        return x + mu_ref[...] * (prev - x)

    xr = shifted(zr_ref, er_ref, mr_ref)
    xk = shifted(zk_ref, ek_ref, mk_ref)
    xv = shifted(zv_ref, ev_ref, mv_ref)
    xg = shifted(zg_ref, eg_ref, mg_ref)
    xwa = shifted(zwa_ref, ewa_ref, mwa_ref)

    row = i * tm + lax.broadcasted_iota(jnp.int32, (tm, 1), 0)
    valid = _row_valid(row, tp, pad, nb)

    w = w0_ref[...] + jnp.dot(jnp.tanh(xwa), w2_ref[...], precision=HIGHEST, preferred_element_type=F32)
    w = -_softplus(-w) - 0.5
    lw_out[...] = -jnp.exp(w)
    a = jax.nn.sigmoid(a0_ref[...] + jnp.dot(xwa, a2_ref[...], precision=HIGHEST, preferred_element_type=F32))
    a_out[...] = a
    g_out[...] = jnp.dot(jax.nn.sigmoid(xg), g2_ref[...], precision=HIGHEST, preferred_element_type=F32)
    r_out[...] = xr
    v_out[...] = jnp.where(valid, xv, 0.0)
    k_out[...] = jnp.where(valid, xk * (1.0 + (a - 1.0) * ka_ref[...]), 0.0)
    kk = xk * kk_ref[...]
    hi = _block_id(lax.broadcasted_iota(jnp.int32, (RW_WIDTH, RW_WIDTH), 0), RW_HEAD)
    hj = _block_id(lax.broadcasted_iota(jnp.int32, (RW_WIDTH, RW_WIDTH), 1), RW_HEAD)
    same = (hi == hj).astype(F32)
    n2 = jnp.dot(kk * kk, same, precision=HIGHEST, preferred_element_type=F32)
    kn = kk / jnp.maximum(jnp.sqrt(n2 + 1e-24), 1e-12)
    kn_out[...] = jnp.where(valid, kn, 0.0)


def _rwkv_pre(z, mu, w0, w2, a0, a2, g2, k_k, k_a, *, nb, tp, pad, tm):
    m = z.shape[0]
    mu_r, mu_k, mu_v = (mu[s:s + RW_WIDTH].reshape(1, RW_WIDTH) for s in (0, RW_WIDTH, 2 * RW_WIDTH))
    o = 3 * RW_WIDTH
    mu_wa = mu[o:o + 128].reshape(1, 128)
    mu_g = jnp.pad(mu[o + 128:o + 128 + RW_GATE_LORA], (0, 256 - RW_GATE_LORA)).reshape(1, 256)
    w2p = jnp.concatenate([w2, jnp.zeros_like(a2)], axis=0)
    a2p = jnp.concatenate([jnp.zeros_like(w2), a2], axis=0)
    g2p = jnp.pad(g2, ((0, 256 - RW_GATE_LORA), (0, 0)))
    kern = functools.partial(_rwkv_pre_kernel, tm=tm, tp=tp, pad=pad, nb=nb)

    def zspec(off, width):
        return pl.BlockSpec((tm, width), lambda i: (i, off // width))

    def full(shape):
        return pl.BlockSpec(shape, lambda i: (0,) * len(shape))

    vec = full((1, RW_WIDTH))
    wide = pl.BlockSpec((tm, RW_WIDTH), lambda i: (i, 0))
    outs = pl.pallas_call(
        kern,
        grid=(m // tm,),
        in_specs=[zspec(Z_R, RW_WIDTH), zspec(Z_K, RW_WIDTH), zspec(Z_V, RW_WIDTH), zspec(Z_GL, 256), zspec(Z_WA, 128),
                  vec, vec, vec, full((1, 256)), full((1, 128)),
                  vec, full((128, RW_WIDTH)), vec, full((128, RW_WIDTH)), full((256, RW_WIDTH)), vec, vec],
        out_specs=[wide] * 7,
        out_shape=[jax.ShapeDtypeStruct((m, RW_WIDTH), F32)] * 7,
        scratch_shapes=[pltpu.VMEM((tm + SUBLANES, RW_WIDTH), F32)] * 3
        + [pltpu.VMEM((tm + SUBLANES, 256), F32), pltpu.VMEM((tm + SUBLANES, 128), F32)],
        compiler_params=_params(("arbitrary",)),
        name="rwkv_pre",
    )(z, z, z, z, z, mu_r, mu_k, mu_v, mu_g, mu_wa,
      w0.reshape(1, -1), w2p, a0.reshape(1, -1), a2p, g2p, k_k.reshape(1, -1), k_a.reshape(1, -1))
    return outs


def _mm(a, b):
    return jnp.dot(a, b, precision=HIGHEST, preferred_element_type=F32)


def _mm_nt(a, b):
    return lax.dot_general(a, b, (((1,), (1,)), ((), ())), precision=HIGHEST, preferred_element_type=F32)


def _unit_lower_inverse(n_mat, blk):
    n = n_mat.shape[0]
    ri = lax.broadcasted_iota(jnp.int32, (n, n), 0)
    ci = lax.broadcasted_iota(jnp.int32, (n, n), 1)
    eye = (ri == ci).astype(F32)
    diag_blk = _block_id(ri, blk) == _block_id(ci, blk)
    x = jnp.where(diag_blk, -n_mat, 0.0)
    off = jnp.where(diag_blk, 0.0, n_mat)
    d_inv = eye + x
    p = x
    steps = int(np.log2(blk))
    for _ in range(steps - 1):
        p = _mm(p, p)
        d_inv = _mm(d_inv, eye + p)
    m2 = -_mm(d_inv, off)
    t = eye + m2
    p = m2
    steps = int(np.log2(RW_CHUNK // blk))
    for _ in range(steps - 1):
        p = _mm(p, p)
        t = _mm(t, eye + p)
    return _mm(t, d_inv)


def _rwkv_rec_kernel(r_ref, lw_ref, k_ref, v_ref, kn_ref, a_ref, g_ref, rk_ref, lng_ref, lnb_ref,
                     o_ref, st_ref, *, chunk):
    c = pl.program_id(2)

    @pl.when(c == 0)
    def _():
        st_ref[...] = jnp.zeros_like(st_ref)

    L = chunk
    r = r_ref[...]
    lw = lw_ref[...]
    k = k_ref[...]
    v = v_ref[...]
    kn = kn_ref[...]
    a = a_ref[...]
    lane = lax.broadcasted_iota(jnp.int32, (1, LANES), 1)
    m0 = (lane < RW_HEAD).astype(F32)
    m1 = 1.0 - m0

    def stack(x):
        return jnp.concatenate([x * m0, x * m1], axis=0)

    tri = (lax.broadcasted_iota(jnp.int32, (L, L), 0) >= lax.broadcasted_iota(jnp.int32, (L, L), 1)).astype(F32)
    cum = _mm(tri, lw)
    c_last = cum[L - 1:L, :]
    e_neg = jnp.exp(-cum)
    beta = kn * a
    al_t = stack(kn * jnp.exp(cum - lw))
    r_t = stack(r * jnp.exp(cum))
    be_h = stack(beta * e_neg)
    k_h = stack(k * e_neg)
    e_end = jnp.exp(c_last - cum)
    be_e = stack(beta * e_end)
    k_e = stack(k * e_end)
    v_s = stack(v)

    n2 = 2 * L
    ri = lax.broadcasted_iota(jnp.int32, (n2, n2), 0)
    ci = lax.broadcasted_iota(jnp.int32, (n2, n2), 1)
    strict = ri > ci
    incl = ri >= ci
    a_ab = jnp.where(strict, _mm_nt(al_t, be_h), 0.0)
    a_ak = jnp.where(strict, _mm_nt(al_t, k_h), 0.0)
    a_rb = jnp.where(incl, _mm_nt(r_t, be_h), 0.0)
    a_rk = jnp.where(incl, _mm_nt(r_t, k_h), 0.0)
    t_inv = _unit_lower_inverse(a_ab, 16)
    al_hat = _mm(t_inv, al_t)
    v_hat = _mm(t_inv, _mm(a_ak, v_s))

    st = st_ref[...]
    u = -(_mm_nt(al_hat, st) + v_hat)
    y = _mm_nt(r_t, st) + _mm(a_rb, u) + _mm(a_rk, v_s)
    st_ref[...] = st * jnp.exp(c_last) + _mm(u.T, be_e) + _mm(v_s.T, k_e)
    y = y[0:L] + y[L:n2]

    def head_mean(x):
        s0 = jnp.sum(x * m0, axis=-1, keepdims=True)
        s1 = jnp.sum(x * m1, axis=-1, keepdims=True)
        return (s0 * m0 + s1 * m1) * (1.0 / RW_HEAD)

    mu = head_mean(y)
    yc = y - mu
    var = head_mean(yc * yc)
    yn = yc * lax.rsqrt(var + RW_GN_EPS) * lng_ref[...] + lnb_ref[...]
    bonus = head_mean(r * k * rk_ref[...]) * RW_HEAD
    o_ref[...] = ((yn + bonus * v) * g_ref[...]).astype(o_ref.dtype)


def _rwkv_rec(pre, r_k, ln_g, ln_b, *, nb, tp):
    r, lw, k, v, kn, a, g = pre
    m = r.shape[0]
    L = RW_CHUNK
    nc = tp // L
    npair = RW_WIDTH // LANES
    kern = functools.partial(_rwkv_rec_kernel, chunk=L)
    blk = pl.BlockSpec((L, LANES), lambda b, h, c: (b * nc + c, h))
    pspec = pl.BlockSpec((1, LANES), lambda b, h, c: (0, h))
    return pl.pallas_call(
        kern,
        grid=(nb, npair, nc),
        in_specs=[blk] * 7 + [pspec] * 3,
        out_specs=blk,
        out_shape=jax.ShapeDtypeStruct((m, RW_WIDTH), BF16),
        scratch_shapes=[pltpu.VMEM((LANES, LANES), F32)],
        compiler_params=_params(("arbitrary", "arbitrary", "arbitrary")),
        name="rwkv_rec",
    )(r, lw, k, v, kn, a, g, r_k.reshape(1, -1).astype(F32), ln_g.reshape(1, -1).astype(F32),
      ln_b.reshape(1, -1).astype(F32))


def _rope_kernel(x_ref, cos_ref, sin_ref, o_ref, *, n_plain, n_rope, scale):
    lane = lax.broadcasted_iota(jnp.int32, (1, LANES), 1)
    cos = cos_ref[...]
    sin = sin_ref[...]
    if n_plain:
        o_ref[:, 0:n_plain * LANES] = (x_ref[:, 0:n_plain * LANES] * scale).astype(o_ref.dtype)
    for h in range(n_plain, n_plain + n_rope):
        x = x_ref[:, h * LANES:(h + 1) * LANES]
        rot = jnp.where(lane < MLA_ROPE // 2, -pltpu.roll(x, LANES - MLA_ROPE // 2, 1),
                        pltpu.roll(x, MLA_ROPE // 2, 1))
        o_ref[:, h * LANES:(h + 1) * LANES] = ((x * cos + rot * sin) * scale).astype(o_ref.dtype)


def _rope(x, col_block, n_plain, n_rope, cos, sin, scale, *, tp, tr):
    m = x.shape[0]
    nrb = tp // tr
    width = (n_plain + n_rope) * LANES
    kern = functools.partial(_rope_kernel, n_plain=n_plain, n_rope=n_rope, scale=scale)
    tab = pl.BlockSpec((tr, LANES), lambda i: (i % nrb, 0))
    return pl.pallas_call(
        kern,
        grid=(m // tr,),
        in_specs=[pl.BlockSpec((tr, width), lambda i: (i, col_block)), tab, tab],
        out_specs=pl.BlockSpec((tr, width), lambda i: (i, 0)),
        out_shape=jax.ShapeDtypeStruct((m, width), BF16),
        compiler_params=_params(("arbitrary",)),
        name="rope",
    )(x, cos, sin)


def _attn_kernel(qn_ref, qr_ref, kn_ref, kr_ref, v_ref, o_ref, *, tq, tk, pad):
    qi = pl.program_id(2)
    q = jnp.concatenate([qn_ref[...], qr_ref[...]], axis=1)
    qpos = qi * tq + lax.broadcasted_iota(jnp.int32, (tq, 1), 0)

    def body(kb, carry):
        m_i, l_i, acc = carry
        off = pl.multiple_of(kb * tk, tk)
        kcat = jnp.concatenate([kn_ref[pl.ds(off, tk), :], kr_ref[pl.ds(off, tk), :]], axis=1)
        s = lax.dot_general(q, kcat, (((1,), (1,)), ((), ())), preferred_element_type=F32)
        kpos = off + lax.broadcasted_iota(jnp.int32, (1, tk), 1)
        s = jnp.where(jnp.logical_and(kpos <= qpos, kpos >= pad), s, NEG_BIG)
        m_new = jnp.maximum(m_i, jnp.max(s, axis=-1, keepdims=True))
        alpha = jnp.exp(m_i - m_new)
        p = jnp.exp(s - m_new)
        l_new = alpha * l_i + jnp.sum(p, axis=-1, keepdims=True)
        acc = alpha * acc + jnp.dot(p.astype(BF16), v_ref[pl.ds(off, tk), :], preferred_element_type=F32)
        return m_new, l_new, acc

    nkb = (qi * tq + tq + tk - 1) // tk
    init = (jnp.full((tq, 1), NEG_BIG, F32), jnp.zeros((tq, 1), F32), jnp.zeros((tq, MLA_V), F32))
    _, l_i, acc = lax.fori_loop(0, nkb, body, init)
    o_ref[...] = (acc / l_i).astype(o_ref.dtype)


def _attention(q, kr, kv, *, nb, tp, pad):
    m = q.shape[0]
    tq = _pick(tp, (384, 256, 128))
    tk = tq
    nq = tp // tq
    kern = functools.partial(_attn_kernel, tq=tq, tk=tk, pad=pad)
    return pl.pallas_call(
        kern,
        grid=(nb, MLA_HEADS, nq),
        in_specs=[pl.BlockSpec((tq, LANES), lambda b, h, i: (b * nq + i, h)),
                  pl.BlockSpec((tq, LANES), lambda b, h, i: (b * nq + i, MLA_HEADS + h)),
                  pl.BlockSpec((tp, LANES), lambda b, h, i: (b, h)),
                  pl.BlockSpec((tp, LANES), lambda b, h, i: (b, 0)),
                  pl.BlockSpec((tp, LANES), lambda b, h, i: (b, MLA_HEADS + h))],
        out_specs=pl.BlockSpec((tq, LANES), lambda b, h, i: (b * nq + i, h)),
        out_shape=jax.ShapeDtypeStruct((m, MLA_WIDTH), BF16),
        compiler_params=_params(("arbitrary", "arbitrary", "arbitrary")),
        name="mla_attention",
    )(q, q, kv, kr, kv)


def _merge_kernel(ya_ref, yb_ref, yc_ref, ga_ref, gb_ref, gc_ref, w_ref, o_ref):
    acc = None
    for n, (y_ref, g_ref) in enumerate(((ya_ref, ga_ref), (yb_ref, gb_ref), (yc_ref, gc_ref))):
        p = jnp.dot(y_ref[...], w_ref[n], preferred_element_type=F32)
        t = jax.nn.sigmoid(g_ref[...]) * p
        acc = t if acc is None else acc + t
    o_ref[...] = acc.astype(o_ref.dtype)


def _merge(ya, yb, yc, z, w_branch, *, tm, tn):
    m = ya.shape[0]
    d = w_branch.shape[2]
    yspec = pl.BlockSpec((tm, HG_WIDTH), lambda i, j: (i, 0))

    def gspec(n):
        return pl.BlockSpec((tm, tn), lambda i, j: (i, (Z_GATES + n * d) // tn + j))

    return pl.pallas_call(
        _merge_kernel,
        grid=(m // tm, d // tn),
        in_specs=[yspec, yspec, yspec, gspec(0), gspec(1), gspec(2),
                  pl.BlockSpec((N_BRANCH, HG_WIDTH, tn), lambda i, j: (0, 0, j))],
        out_specs=pl.BlockSpec((tm, tn), lambda i, j: (i, j)),
        out_shape=jax.ShapeDtypeStruct((m, d), BF16),
        compiler_params=_params(("arbitrary", "arbitrary")),
        name="merge",
    )(ya, yb, yc, z, z, z, w_branch)


def _matmul_res_kernel(x_ref, w_ref, r_ref, o_ref):
    o_ref[...] = r_ref[...] + jnp.dot(x_ref[...], w_ref[...], preferred_element_type=F32)


def _matmul_res(x, w, res, *, tm, tn):
    m, k = x.shape
    n = w.shape[1]
    return pl.pallas_call(
        _matmul_res_kernel,
        grid=(m // tm, n // tn),
        in_specs=[pl.BlockSpec((tm, k), lambda i, j: (i, 0)),
                  pl.BlockSpec((k, tn), lambda i, j: (0, j)),
                  pl.BlockSpec((tm, tn), lambda i, j: (i, j))],
        out_specs=pl.BlockSpec((tm, tn), lambda i, j: (i, j)),
        out_shape=jax.ShapeDtypeStruct((m, n), F32),
        compiler_params=_params(("arbitrary", "arbitrary")),
        name="matmul_res",
    )(x, w, res)


def _ffn_up_kernel(x_ref, g_ref, wg_ref, wv_ref, cwg_ref, cwv_ref, cbg_ref, cbv_ref, o_ref,
                   xn_ref, eg_ref, ev_ref, pg_ref, pv_ref, *, tm, tp, pad, nb, eps):
    i = pl.program_id(0)
    j = pl.program_id(1)

    @pl.when(j == 0)
    def _():
        xn_ref[...] = _normed_rows(x_ref[...], g_ref[...], eps, i * tm, tp, pad, nb).astype(BF16)

    @pl.when(i == 0)
    def _():
        pg_ref[j] = jnp.zeros(pg_ref.shape[1:], F32)
        pv_ref[j] = jnp.zeros(pv_ref.shape[1:], F32)

    def conv(w_ref, e_ref, p_ref, cw_ref, cb_ref):
        u = jnp.dot(xn_ref[...], w_ref[...], preferred_element_type=F32)
        e_ref[0:SUBLANES, :] = p_ref[j]
        e_ref[SUBLANES:SUBLANES + tm, :] = u
        p_ref[j] = u[tm - SUBLANES:tm, :]
        u1 = e_ref[SUBLANES - 1:SUBLANES - 1 + tm, :]
        u2 = e_ref[SUBLANES - 2:SUBLANES - 2 + tm, :]
        cw = cw_ref[...]
        return cw[0:1, :] * u2 + cw[1:2, :] * u1 + cw[2:3, :] * u + cb_ref[...]

    gate = conv(wg_ref, eg_ref, pg_ref, cwg_ref, cbg_ref)
    val = conv(wv_ref, ev_ref, pv_ref, cwv_ref, cbv_ref)
    o_ref[...] = (gate * jax.nn.sigmoid(gate) * val).astype(o_ref.dtype)


def _ffn_up(h, g, w_up, conv_w, conv_b, *, tp, pad, nb, tm, tf):
    m, d = h.shape
    dff = w_up.shape[1] // 2
    nj = dff // tf
    kern = functools.partial(_ffn_up_kernel, tm=tm, tp=tp, pad=pad, nb=nb, eps=EPS)
    cb = conv_b.reshape(1, -1).astype(F32)
    return pl.pallas_call(
        kern,
        grid=(m // tm, nj),
        in_specs=[pl.BlockSpec((tm, d), lambda i, j: (i, 0)),
                  pl.BlockSpec((1, d), lambda i, j: (0, 0)),
                  pl.BlockSpec((d, tf), lambda i, j: (0, j)),
                  pl.BlockSpec((d, tf), lambda i, j: (0, nj + j)),
                  pl.BlockSpec((CONV_W, tf), lambda i, j: (0, j)),
                  pl.BlockSpec((CONV_W, tf), lambda i, j: (0, nj + j)),
                  pl.BlockSpec((1, tf), lambda i, j: (0, j)),
                  pl.BlockSpec((1, tf), lambda i, j: (0, nj + j))],
        out_specs=pl.BlockSpec((tm, tf), lambda i, j: (i, j)),
        out_shape=jax.ShapeDtypeStruct((m, dff), BF16),
        scratch_shapes=[pltpu.VMEM((tm, d), BF16),
                        pltpu.VMEM((tm + SUBLANES, tf), F32), pltpu.VMEM((tm + SUBLANES, tf), F32),
                        pltpu.VMEM((nj, SUBLANES, tf), F32), pltpu.VMEM((nj, SUBLANES, tf), F32)],
        compiler_params=_params(("arbitrary", "arbitrary")),
        name="ffn_up",
    )(h, g.reshape(1, d).astype(F32), w_up, w_up, conv_w.astype(F32), conv_w.astype(F32), cb, cb)


def _final_norm_kernel(x_ref, g_ref, o_ref):
    x = x_ref[0]
    o_ref[0] = x * lax.rsqrt(jnp.mean(x * x, axis=-1, keepdims=True) + EPS) * g_ref[...]


def _final_norm(h3, g, *, skip, seq):
    nb, tp, d = h3.shape
    tr = LANES
    return pl.pallas_call(
        _final_norm_kernel,
        grid=(nb, seq // tr),
        in_specs=[pl.BlockSpec((1, tr, d), lambda b, i: (b, skip // tr + i, 0)),
                  pl.BlockSpec((1, d), lambda b, i: (0, 0))],
        out_specs=pl.BlockSpec((1, tr, d), lambda b, i: (b, i, 0)),
        out_shape=jax.ShapeDtypeStruct((nb, seq, d), F32),
        compiler_params=_params(("arbitrary", "arbitrary")),
        name="final_norm",
    )(h3, g.reshape(1, d).astype(F32))


def _layout_w_in(w):
    d = w.shape[0]
    o = 0
    parts = {}
    for name, width in (("hg4", 4 * HG_WIDTH), ("rkv", 3 * RW_WIDTH), ("wl", RW_DECAY_LORA), ("al", RW_AAA_LORA),
                        ("gl", RW_GATE_LORA), ("cq", MLA_Q_RANK), ("ckv", MLA_KV_RANK), ("kr", MLA_ROPE),
                        ("gates", N_BRANCH * d)):
        parts[name] = w[:, o:o + width]
        o += width
    zeros = lambda n: jnp.zeros((d, n), w.dtype)
    cols = [parts["hg4"], parts["rkv"], parts["cq"], parts["ckv"],
            parts["gl"], zeros(256 - RW_GATE_LORA), parts["wl"], parts["al"],
            parts["kr"], zeros(LANES - MLA_ROPE), parts["gates"]]
    return jnp.concatenate(cols, axis=1).astype(BF16)


def _layout_w_uq(w):
    r = w.shape[0]
    w = w.reshape(r, MLA_HEADS, MLA_NOPE + MLA_ROPE)
    nope = w[:, :, :MLA_NOPE].reshape(r, MLA_HEADS * MLA_NOPE)
    rope = jnp.pad(w[:, :, MLA_NOPE:], ((0, 0), (0, 0), (0, LANES - MLA_ROPE))).reshape(r, MLA_HEADS * LANES)
    return jnp.concatenate([nope, rope], axis=1).astype(BF16)


def _layout_w_ukv(w):
    r = w.shape[0]
    w = w.reshape(r, MLA_HEADS, MLA_NOPE + MLA_V)
    return jnp.concatenate([w[:, :, :MLA_NOPE].reshape(r, -1), w[:, :, MLA_NOPE:].reshape(r, -1)], axis=1).astype(BF16)


def _rope_tables(tp, pad):
    inv = ROPE_BASE ** (-jnp.arange(0, MLA_ROPE, 2, dtype=F32) / MLA_ROPE)
    pos = jnp.arange(tp, dtype=F32) - float(pad)
    ang = pos[:, None] * inv[None, :]
    z = jnp.zeros((tp, LANES - MLA_ROPE), F32)
    cos = jnp.concatenate([jnp.cos(ang), jnp.cos(ang), z], axis=1)
    sin = jnp.concatenate([jnp.sin(ang), jnp.sin(ang), z], axis=1)
    return cos, sin


def kernel(x, meta_tokens, norm1_g, w_in, hg_lb_logits, hg_norm_g, rw_mu, rw_w0, rw_w2, rw_a0, rw_a2, rw_g2, rw_k_k, rw_k_a, rw_r_k, rw_ln_g, rw_ln_b, mla_q_norm_g, mla_w_uq, mla_kv_norm_g, mla_w_ukv, w_branch, w_out, norm2_g, ffn_w_up, ffn_conv_w, ffn_conv_b, ffn_w_down, final_norm_g):
    nb, seq, d = x.shape
    depth = w_in.shape[0]
    t = N_META + seq
    pad = (-N_META) % LANES
    tp = pad + t
    m = nb * tp
    tm = _pick(m, (768, 384, 256, 128))
    geo = dict(tp=tp, pad=pad, nb=nb)

    meta = jnp.broadcast_to(meta_tokens[None].astype(x.dtype), (nb, N_META, d))
    h = jnp.concatenate([jnp.zeros((nb, pad, d), x.dtype), meta, x], axis=1).reshape(m, d)
    cos, sin = _rope_tables(tp, pad)
    p_lb = jax.nn.softmax(hg_lb_logits.astype(F32), axis=0)
    lower_bounds = jnp.cumsum(p_lb, axis=0) - p_lb[0]
    scale = float((MLA_NOPE + MLA_ROPE) ** -0.5)

    for l in range(depth):
        z = _norm_matmul(h, 0, d, norm1_g[l], _layout_w_in(w_in[l]), F32, tm=tm, tn=512, **geo)
        y_a = _hgrn2(z, lower_bounds[l], hg_norm_g[l], **geo)
        pre = _rwkv_pre(z, rw_mu[l], rw_w0[l], rw_w2[l], rw_a0[l], rw_a2[l], rw_g2[l], rw_k_k[l], rw_k_a[l],
                        tm=_pick(m, (384, 256, 128)), **geo)
        y_b = _rwkv_rec(pre, rw_r_k[l], rw_ln_g[l], rw_ln_b[l], nb=nb, tp=tp)
        q = _norm_matmul(z, Z_CQ // MLA_Q_RANK, MLA_Q_RANK, mla_q_norm_g[l], _layout_w_uq(mla_w_uq[l]), F32,
                         tm=tm, tn=512, tp=tp, pad=0, nb=nb)
        kv = _norm_matmul(z, Z_CKV // MLA_KV_RANK, MLA_KV_RANK, mla_kv_norm_g[l], _layout_w_ukv(mla_w_ukv[l]), BF16,
                          tm=tm, tn=512, tp=tp, pad=0, nb=nb)
        tr = _pick(tp, (384, 256, 128))
        q_cat = _rope(q, 0, MLA_HEADS, MLA_HEADS, cos, sin, scale, tp=tp, tr=tr)
        k_rope = _rope(z, Z_KR // LANES, 0, 1, cos, sin, 1.0, tp=tp, tr=tr)
        y_c = _attention(q_cat, k_rope, kv, **geo)
        merged = _merge(y_a, y_b, y_c, z, w_branch[l].astype(BF16), tm=tm, tn=512)
        h = _matmul_res(merged, w_out[l].astype(BF16), h, tm=tm, tn=512)
        act = _ffn_up(h, norm2_g[l], ffn_w_up[l].astype(BF16), ffn_conv_w[l], ffn_conv_b[l], tm=tm, tf=512, **geo)
        h = _matmul_res(act, ffn_w_down[l].astype(BF16), h, tm=tm, tn=512)
    return _final_norm(h.reshape(nb, tp, d), final_norm_g, skip=pad + N_META, seq=seq)
```

```python
import functools

import numpy as np
import jax
import jax.numpy as jnp
from jax import lax
from jax.experimental import pallas as pl
from jax.experimental.pallas import tpu as pltpu

F32 = jnp.float32
BF16 = jnp.bfloat16
HIGHEST = lax.Precision.HIGHEST

N_META = 16
EPS = 1e-6
NEG_BIG = -1e30
F_FLOOR = 1e-30
HG_HEADS = 8
HG_DIM = 128
HG_WIDTH = HG_HEADS * HG_DIM
RW_HEAD = 64
RW_HEADS = 16
RW_WIDTH = RW_HEADS * RW_HEAD
RW_DECAY_LORA = 64
RW_AAA_LORA = 64
RW_GATE_LORA = 160
RW_GN_EPS = 64e-5
MLA_HEADS = 8
MLA_Q_RANK = 512
MLA_KV_RANK = 512
MLA_NOPE = 128
MLA_ROPE = 64
MLA_V = 128
MLA_WIDTH = MLA_HEADS * MLA_V
ROPE_BASE = 10000.0
CONV_W = 3
N_BRANCH = 3

LANES = 128
SUBLANES = 8
VMEM_LIMIT = 56 * 1024 * 1024

Z_HQ, Z_HF, Z_HI, Z_HG = 0, 1024, 2048, 3072
Z_R, Z_K, Z_V = 4096, 5120, 6144
Z_CQ, Z_CKV = 7168, 7680
Z_GL = 8192
Z_WA = 8448
Z_KR = 8576
Z_GATES = 8704
Z_COLS = Z_GATES + N_BRANCH * 2048

RW_CHUNK = 64
HG_CHUNK = 64
HG_BASE = 8
HG_PASSES = 1


def _params(sem):
    return pltpu.CompilerParams(dimension_semantics=sem, vmem_limit_bytes=VMEM_LIMIT)


def _pick(n, cands):
    for c in cands:
        if n % c == 0:
            return c
    raise ValueError(f"no tile for {n} in {cands}")


def _block_id(idx, size):
    return lax.shift_right_logical(idx, int(np.log2(size)))


def _softplus(x):
    return jnp.maximum(x, 0.0) + jnp.log(1.0 + jnp.exp(-jnp.abs(x)))


def _row_valid(row, tp, pad, nb):
    valid = None
    for b in range(nb):
        ok = jnp.logical_or(row < b * tp, row >= b * tp + pad)
        valid = ok if valid is None else jnp.logical_and(valid, ok)
    return valid


def _normed_rows(x, g, eps, row0, tp, pad, nb):
    ms = jnp.mean(x * x, axis=-1, keepdims=True)
    y = x * lax.rsqrt(ms + eps) * g
    if pad:
        row = row0 + lax.broadcasted_iota(jnp.int32, (x.shape[0], 1), 0)
        y = jnp.where(_row_valid(row, tp, pad, nb), y, 0.0)
    return y


def _norm_matmul_kernel(x_ref, g_ref, w_ref, o_ref, xn_ref, *, tm, tp, pad, nb, eps):
    i = pl.program_id(0)

    @pl.when(pl.program_id(1) == 0)
    def _():
        xn_ref[...] = _normed_rows(x_ref[...], g_ref[...], eps, i * tm, tp, pad, nb).astype(BF16)

    o_ref[...] = jnp.dot(xn_ref[...], w_ref[...], preferred_element_type=F32).astype(o_ref.dtype)


def _norm_matmul(x, col_block, k, g, w, out_dtype, *, tp, pad, nb, tm, tn):
    m = x.shape[0]
    n = w.shape[1]
    kern = functools.partial(_norm_matmul_kernel, tm=tm, tp=tp, pad=pad, nb=nb, eps=EPS)
    return pl.pallas_call(
        kern,
        grid=(m // tm, n // tn),
        in_specs=[pl.BlockSpec((tm, k), lambda i, j: (i, col_block)),
                  pl.BlockSpec((1, k), lambda i, j: (0, 0)),
                  pl.BlockSpec((k, tn), lambda i, j: (0, j))],
        out_specs=pl.BlockSpec((tm, tn), lambda i, j: (i, j)),
        out_shape=jax.ShapeDtypeStruct((m, n), out_dtype),
        scratch_shapes=[pltpu.VMEM((tm, k), BF16)],
        compiler_params=_params(("arbitrary", "arbitrary")),
        name="norm_matmul",
    )(x, g.reshape(1, k).astype(F32), w)


def _hgrn_kernel(q_ref, f_ref, i_ref, g_ref, lb_ref, ng_ref, o_ref, st_ref, *, chunk, pad, nheads):
    c = pl.program_id(1)

    @pl.when(c == 0)
    def _():
        st_ref[...] = jnp.zeros_like(st_ref)

    L = chunk
    base = HG_BASE
    heads = range(nheads)
    hs = [slice(h * HG_DIM, (h + 1) * HG_DIM) for h in heads]
    q = q_ref[...]
    zf = f_ref[...]
    v = i_ref[...]
    lb = lb_ref[...]
    row = lax.broadcasted_iota(jnp.int32, (L, 1), 0)
    valid = c * L + row >= pad
    f = lb + (1.0 - lb) * jax.nn.sigmoid(zf)
    logf = jnp.where(valid, jnp.log(jnp.maximum(f, F_FLOOR)), 0.0)
    k = jnp.where(valid, (1.0 - lb) * jax.nn.sigmoid(-zf), 0.0)
    tri = (row >= lax.broadcasted_iota(jnp.int32, (L, L), 1)).astype(F32)
    b = _mm(tri, logf)

    r = row & (base - 1)
    o = [jnp.zeros((L, HG_DIM), F32) for _ in heads]
    for lag in range(base):
        ks, bs, vs = (k, b, v) if lag == 0 else tuple(pltpu.roll(x, lag, 0) for x in (k, b, v))
        ok = r >= lag
        term = q * ks * jnp.exp(jnp.where(ok, b - bs, 0.0))
        for h in heads:
            a = jnp.sum(term[:, hs[h]], axis=-1, keepdims=True)
            o[h] = o[h] + jnp.where(ok, a, 0.0) * vs[:, hs[h]]

    size = L
    while size > base:
        half = size // 2
        blocks = []
        for lo in range(0, L, size):
            mid, hi = lo + half, lo + size
            b_mid = b[mid - 1:mid, :]
            blocks.append((lo, mid, q[mid:hi] * jnp.exp(b[mid:hi] - b_mid), k[lo:mid] * jnp.exp(b_mid - b[lo:mid])))
        sc = [[_mm_nt(qg[:, hs[h]], kg[:, hs[h]], HG_PASSES) for h in heads] for (_, _, qg, kg) in blocks]
        pv = [[_mm(sc[bi][h], v[lo:mid, hs[h]], HG_PASSES) for h in heads] for bi, (lo, mid, _, _) in enumerate(blocks)]
        zero = jnp.zeros((half, HG_DIM), F32)
        for h in heads:
            o[h] = o[h] + jnp.concatenate([x for bi in range(len(blocks)) for x in (zero, pv[bi][h])], axis=0)
        size = half

    b_last = b[L - 1:L, :]
    q_dec = q * jnp.exp(b)
    k_end = k * jnp.exp(b_last - b)
    w_end = jnp.exp(b_last)
    st = [st_ref[h] for h in heads]
    o_st = [_mm_nt(q_dec[:, hs[h]], st[h], HG_PASSES) for h in heads]
    upd = [_mm(v[:, hs[h]].T, k_end[:, hs[h]], HG_PASSES) for h in heads]
    for h in heads:
        st_ref[h] = st[h] * w_end[:, hs[h]] + upd[h]

    zg = g_ref[...]
    gate = zg * jax.nn.sigmoid(zg)
    ng = ng_ref[...]
    for h in heads:
        y = o[h] + o_st[h]
        y = y * lax.rsqrt(jnp.mean(y * y, axis=-1, keepdims=True) + EPS) * ng[:, hs[h]]
        o_ref[:, hs[h]] = (y * gate[:, hs[h]]).astype(o_ref.dtype)


def _hgrn2(z, lb, norm_g, *, nb, tp, pad):
    m = z.shape[0]
    nc = tp // HG_CHUNK
    kern = functools.partial(_hgrn_kernel, chunk=HG_CHUNK, pad=pad, nheads=HG_HEADS)

    def zspec(off):
        return pl.BlockSpec((HG_CHUNK, HG_WIDTH), lambda b, c: (b * nc + c, off // HG_WIDTH))

    pspec = pl.BlockSpec((1, HG_WIDTH), lambda b, c: (0, 0))
    return pl.pallas_call(
        kern,
        grid=(nb, nc),
        in_specs=[zspec(Z_HQ), zspec(Z_HF), zspec(Z_HI), zspec(Z_HG), pspec, pspec],
        out_specs=pl.BlockSpec((HG_CHUNK, HG_WIDTH), lambda b, c: (b * nc + c, 0)),
        out_shape=jax.ShapeDtypeStruct((m, HG_WIDTH), BF16),
        scratch_shapes=[pltpu.VMEM((HG_HEADS, HG_DIM, HG_DIM), F32)],
        compiler_params=_params(("arbitrary", "arbitrary")),
        name="hgrn2",
    )(z, z, z, z, lb.reshape(1, HG_WIDTH).astype(F32), norm_g.reshape(1, HG_WIDTH).astype(F32))


def _rwkv_pre_kernel(zr_ref, zk_ref, zv_ref, zg_ref, zwa_ref,
                     mr_ref, mk_ref, mv_ref, mg_ref, mwa_ref,
                     w0_ref, w2_ref, a0_ref, a2_ref, g2_ref, kk_ref, ka_ref,
                     r_out, lw_out, k_out, v_out, kn_out, a_out, g_out,
                     er_ref, ek_ref, ev_ref, eg_ref, ewa_ref, *, tm, tp, pad, nb):
    i = pl.program_id(0)

    def shifted(z_ref, e_ref, mu_ref):
        @pl.when(i == 0)
        def _():
            e_ref[0:SUBLANES, :] = jnp.zeros((SUBLANES, e_ref.shape[1]), F32)

        x = z_ref[...]
        e_ref[SUBLANES:SUBLANES + tm, :] = x
        prev = e_ref[SUBLANES - 1:SUBLANES - 1 + tm, :]
        e_ref[0:SUBLANES, :] = x[tm - SUBLANES:tm, :]
        return x + mu_ref[...] * (prev - x)

    xr = shifted(zr_ref, er_ref, mr_ref)
    xk = shifted(zk_ref, ek_ref, mk_ref)
    xv = shifted(zv_ref, ev_ref, mv_ref)
    xg = shifted(zg_ref, eg_ref, mg_ref)
    xwa = shifted(zwa_ref, ewa_ref, mwa_ref)

    row = i * tm + lax.broadcasted_iota(jnp.int32, (tm, 1), 0)
    valid = _row_valid(row, tp, pad, nb)

    w = w0_ref[...] + jnp.dot(jnp.tanh(xwa), w2_ref[...], precision=HIGHEST, preferred_element_type=F32)
    w = -_softplus(-w) - 0.5
    lw_out[...] = -jnp.exp(w)
    a = jax.nn.sigmoid(a0_ref[...] + jnp.dot(xwa, a2_ref[...], precision=HIGHEST, preferred_element_type=F32))
    a_out[...] = a
    g_out[...] = jnp.dot(jax.nn.sigmoid(xg), g2_ref[...], precision=HIGHEST, preferred_element_type=F32)
    r_out[...] = xr
    v_out[...] = jnp.where(valid, xv, 0.0)
    k_out[...] = jnp.where(valid, xk * (1.0 + (a - 1.0) * ka_ref[...]), 0.0)
    kk = xk * kk_ref[...]
    hi = _block_id(lax.broadcasted_iota(jnp.int32, (RW_WIDTH, RW_WIDTH), 0), RW_HEAD)
    hj = _block_id(lax.broadcasted_iota(jnp.int32, (RW_WIDTH, RW_WIDTH), 1), RW_HEAD)
    same = (hi == hj).astype(F32)
    n2 = jnp.dot(kk * kk, same, precision=HIGHEST, preferred_element_type=F32)
    kn = kk / jnp.maximum(jnp.sqrt(n2 + 1e-24), 1e-12)
    kn_out[...] = jnp.where(valid, kn, 0.0)


def _rwkv_pre(z, mu, w0, w2, a0, a2, g2, k_k, k_a, *, nb, tp, pad, tm):
    m = z.shape[0]
    mu_r, mu_k, mu_v = (mu[s:s + RW_WIDTH].reshape(1, RW_WIDTH) for s in (0, RW_WIDTH, 2 * RW_WIDTH))
    o = 3 * RW_WIDTH
    mu_wa = mu[o:o + 128].reshape(1, 128)
    mu_g = jnp.pad(mu[o + 128:o + 128 + RW_GATE_LORA], (0, 256 - RW_GATE_LORA)).reshape(1, 256)
    w2p = jnp.concatenate([w2, jnp.zeros_like(a2)], axis=0)
    a2p = jnp.concatenate([jnp.zeros_like(w2), a2], axis=0)
    g2p = jnp.pad(g2, ((0, 256 - RW_GATE_LORA), (0, 0)))
    kern = functools.partial(_rwkv_pre_kernel, tm=tm, tp=tp, pad=pad, nb=nb)

    def zspec(off, width):
        return pl.BlockSpec((tm, width), lambda i: (i, off // width))

    def full(shape):
        return pl.BlockSpec(shape, lambda i: (0,) * len(shape))

    vec = full((1, RW_WIDTH))
    wide = pl.BlockSpec((tm, RW_WIDTH), lambda i: (i, 0))
    outs = pl.pallas_call(
        kern,
        grid=(m // tm,),
        in_specs=[zspec(Z_R, RW_WIDTH), zspec(Z_K, RW_WIDTH), zspec(Z_V, RW_WIDTH), zspec(Z_GL, 256), zspec(Z_WA, 128),
                  vec, vec, vec, full((1, 256)), full((1, 128)),
                  vec, full((128, RW_WIDTH)), vec, full((128, RW_WIDTH)), full((256, RW_WIDTH)), vec, vec],
        out_specs=[wide] * 7,
        out_shape=[jax.ShapeDtypeStruct((m, RW_WIDTH), F32)] * 7,
        scratch_shapes=[pltpu.VMEM((tm + SUBLANES, RW_WIDTH), F32)] * 3
        + [pltpu.VMEM((tm + SUBLANES, 256), F32), pltpu.VMEM((tm + SUBLANES, 128), F32)],
        compiler_params=_params(("arbitrary",)),
        name="rwkv_pre",
    )(z, z, z, z, z, mu_r, mu_k, mu_v, mu_g, mu_wa,
      w0.reshape(1, -1), w2p, a0.reshape(1, -1), a2p, g2p, k_k.reshape(1, -1), k_a.reshape(1, -1))
    return outs


_NN = (((1,), (0,)), ((), ()))
_NT = (((1,), (1,)), ((), ()))


def _split2(a):
    hi = a.astype(BF16)
    return hi, (a - hi.astype(F32)).astype(BF16)


def _dot(a, b, dims, passes):
    if passes == 6:
        return lax.dot_general(a, b, dims, precision=HIGHEST, preferred_element_type=F32)
    if passes == 1:
        return lax.dot_general(a.astype(BF16), b.astype(BF16), dims, preferred_element_type=F32)
    ah, al = _split2(a)
    bh, bl = _split2(b)
    out = lax.dot_general(ah, bh, dims, preferred_element_type=F32)
    out = out + lax.dot_general(ah, bl, dims, preferred_element_type=F32)
    return out + lax.dot_general(al, bh, dims, preferred_element_type=F32)


def _mm(a, b, passes=6):
    return _dot(a, b, _NN, passes)


def _mm_nt(a, b, passes=6):
    return _dot(a, b, _NT, passes)


def _unit_lower_inverse(n_mats, blk, passes):
    n = n_mats[0].shape[0]
    ri = lax.broadcasted_iota(jnp.int32, (n, n), 0)
    ci = lax.broadcasted_iota(jnp.int32, (n, n), 1)
    eye = (ri == ci).astype(F32)
    diag_blk = _block_id(ri, blk) == _block_id(ci, blk)
    x = [jnp.where(diag_blk, -m, 0.0) for m in n_mats]
    off = [jnp.where(diag_blk, 0.0, m) for m in n_mats]
    d_inv = [eye + xx for xx in x]
    p = x
    for _ in range(int(np.log2(blk)) - 1):
        p = [_mm(pp, pp, passes) for pp in p]
        d_inv = [_mm(d, eye + pp, passes) for d, pp in zip(d_inv, p)]
    m2 = [-_mm(d, o, passes) for d, o in zip(d_inv, off)]
    t = [eye + mm for mm in m2]
    p = m2
    for _ in range(int(np.log2(RW_CHUNK // blk)) - 1):
        p = [_mm(pp, pp, passes) for pp in p]
        t = [_mm(tt, eye + pp, passes) for tt, pp in zip(t, p)]
    return [_mm(tt, d, passes) for tt, d in zip(t, d_inv)]


RW_PASS_SCORES = 1
RW_PASS_INVERSE = 1
RW_PASS_APPLY = 1
RW_PASS_STATE = 1


def _rwkv_rec_kernel(r_ref, lw_ref, k_ref, v_ref, kn_ref, a_ref, g_ref, rk_ref, lng_ref, lnb_ref,
                     o_ref, st_ref, *, chunk, npair):
    @pl.when(pl.program_id(1) == 0)
    def _():
        st_ref[...] = jnp.zeros_like(st_ref)

    L = chunk
    n2 = 2 * L
    lane = lax.broadcasted_iota(jnp.int32, (1, LANES), 1)
    m0 = (lane < RW_HEAD).astype(F32)
    m1 = 1.0 - m0
    tri = (lax.broadcasted_iota(jnp.int32, (L, L), 0) >= lax.broadcasted_iota(jnp.int32, (L, L), 1)).astype(F32)
    ri = lax.broadcasted_iota(jnp.int32, (n2, n2), 0)
    ci = lax.broadcasted_iota(jnp.int32, (n2, n2), 1)
    strict = ri > ci
    incl = ri >= ci
    cum_all = _mm(tri, lw_ref[...])

    def stack(x):
        return jnp.concatenate([x * m0, x * m1], axis=0)

    def head_mean(x):
        s0 = jnp.sum(x * m0, axis=-1, keepdims=True)
        s1 = jnp.sum(x * m1, axis=-1, keepdims=True)
        return (s0 * m0 + s1 * m1) * (1.0 / RW_HEAD)

    pairs = range(npair)
    sls = [slice(p * LANES, (p + 1) * LANES) for p in pairs]
    ops = []
    for sl in sls:
        r = r_ref[:, sl]
        lw = lw_ref[:, sl]
        k = k_ref[:, sl]
        kn = kn_ref[:, sl]
        cum = cum_all[:, sl]
        c_last = cum[L - 1:L, :]
        e_neg = jnp.exp(-cum)
        e_end = jnp.exp(c_last - cum)
        beta = kn * a_ref[:, sl]
        ops.append(dict(
            al_t=stack(kn * jnp.exp(cum - lw)),
            r_t=stack(r * jnp.exp(cum)),
            be_h=stack(beta * e_neg),
            k_h=stack(k * e_neg),
            be_e=stack(beta * e_end),
            k_e=stack(k * e_end),
            v_s=stack(v_ref[:, sl]),
            w_end=jnp.exp(c_last)))

    a_ab = [jnp.where(strict, _mm_nt(o["al_t"], o["be_h"], RW_PASS_INVERSE), 0.0) for o in ops]
    a_ak = [jnp.where(strict, _mm_nt(o["al_t"], o["k_h"], RW_PASS_SCORES), 0.0) for o in ops]
    a_rb = [jnp.where(incl, _mm_nt(o["r_t"], o["be_h"], RW_PASS_SCORES), 0.0) for o in ops]
    a_rk = [jnp.where(incl, _mm_nt(o["r_t"], o["k_h"], RW_PASS_SCORES), 0.0) for o in ops]
    t_inv = _unit_lower_inverse(a_ab, 16, RW_PASS_INVERSE)
    al_hat = [_mm(t, o["al_t"], RW_PASS_APPLY) for t, o in zip(t_inv, ops)]
    g_v = [_mm(a, o["v_s"], RW_PASS_APPLY) for a, o in zip(a_ak, ops)]
    v_hat = [_mm(t, g, RW_PASS_APPLY) for t, g in zip(t_inv, g_v)]
    y_loc = [_mm(a, o["v_s"], RW_PASS_APPLY) for a, o in zip(a_rk, ops)]

    st = [st_ref[p] for p in pairs]
    u = [-(_mm_nt(ah, s, RW_PASS_APPLY) + vh) for ah, s, vh in zip(al_hat, st, v_hat)]
    y_st = [_mm_nt(o["r_t"], s, RW_PASS_APPLY) for o, s in zip(ops, st)]
    y_u = [_mm(a, uu, RW_PASS_APPLY) for a, uu in zip(a_rb, u)]
    s_u = [_mm(uu.T, o["be_e"], RW_PASS_STATE) for uu, o in zip(u, ops)]
    s_v = [_mm(o["v_s"].T, o["k_e"], RW_PASS_STATE) for o in ops]
    for p in pairs:
        st_ref[p] = st[p] * ops[p]["w_end"] + s_u[p] + s_v[p]

    for p, sl in enumerate(sls):
        y = y_st[p] + y_u[p] + y_loc[p]
        y = y[0:L] + y[L:n2]
        r = r_ref[:, sl]
        k = k_ref[:, sl]
        v = v_ref[:, sl]
        mu = head_mean(y)
        yc = y - mu
        var = head_mean(yc * yc)
        yn = yc * lax.rsqrt(var + RW_GN_EPS) * lng_ref[:, sl] + lnb_ref[:, sl]
        bonus = head_mean(r * k * rk_ref[:, sl]) * RW_HEAD
        o_ref[:, sl] = ((yn + bonus * v) * g_ref[:, sl]).astype(o_ref.dtype)


def _rwkv_rec(pre, r_k, ln_g, ln_b, *, nb, tp):
    r, lw, k, v, kn, a, g = pre
    m = r.shape[0]
    L = RW_CHUNK
    nc = tp // L
    npair = RW_WIDTH // LANES
    kern = functools.partial(_rwkv_rec_kernel, chunk=L, npair=npair)
    blk = pl.BlockSpec((L, RW_WIDTH), lambda b, c: (b * nc + c, 0))
    pspec = pl.BlockSpec((1, RW_WIDTH), lambda b, c: (0, 0))
    return pl.pallas_call(
        kern,
        grid=(nb, nc),
        in_specs=[blk] * 7 + [pspec] * 3,
        out_specs=blk,
        out_shape=jax.ShapeDtypeStruct((m, RW_WIDTH), BF16),
        scratch_shapes=[pltpu.VMEM((npair, LANES, LANES), F32)],
        compiler_params=_params(("arbitrary", "arbitrary")),
        name="rwkv_rec",
    )(r, lw, k, v, kn, a, g, r_k.reshape(1, -1).astype(F32), ln_g.reshape(1, -1).astype(F32),
      ln_b.reshape(1, -1).astype(F32))


def _rope_kernel(x_ref, cos_ref, sin_ref, o_ref, *, n_plain, n_rope, scale, is_query, tr, nrb, pad):
    lane = lax.broadcasted_iota(jnp.int32, (1, LANES), 1)
    cos = cos_ref[...]
    sin = sin_ref[...]
    if is_query:
        bias = jnp.ones((tr, 1), F32)
    else:
        pos = (pl.program_id(0) % nrb) * tr + lax.broadcasted_iota(jnp.int32, (tr, 1), 0)
        bias = jnp.where(pos < pad, NEG_BIG, 0.0)
    if n_plain:
        o_ref[:, 0:n_plain * LANES] = (x_ref[:, 0:n_plain * LANES] * scale).astype(o_ref.dtype)
    for h in range(n_plain, n_plain + n_rope):
        x = x_ref[:, h * LANES:(h + 1) * LANES]
        rot = jnp.where(lane < MLA_ROPE // 2, -pltpu.roll(x, LANES - MLA_ROPE // 2, 1),
                        pltpu.roll(x, MLA_ROPE // 2, 1))
        y = (x * cos + rot * sin) * scale
        o_ref[:, h * LANES:(h + 1) * LANES] = jnp.where(lane == MLA_ROPE, bias, y).astype(o_ref.dtype)


def _rope(x, col_block, n_plain, n_rope, cos, sin, scale, is_query, *, tp, tr, pad):
    m = x.shape[0]
    nrb = tp // tr
    width = (n_plain + n_rope) * LANES
    kern = functools.partial(_rope_kernel, n_plain=n_plain, n_rope=n_rope, scale=scale, is_query=is_query,
                             tr=tr, nrb=nrb, pad=pad)
    tab = pl.BlockSpec((tr, LANES), lambda i: (i % nrb, 0))
    return pl.pallas_call(
        kern,
        grid=(m // tr,),
        in_specs=[pl.BlockSpec((tr, width), lambda i: (i, col_block)), tab, tab],
        out_specs=pl.BlockSpec((tr, width), lambda i: (i, 0)),
        out_shape=jax.ShapeDtypeStruct((m, width), BF16),
        compiler_params=_params(("arbitrary",)),
        name="rope",
    )(x, cos, sin)


def _attn_kernel(qn_ref, qr_ref, kn_ref, kr_ref, v_ref, bias_ref, o_ref, *, t):
    qi = pl.program_id(2)
    q = jnp.concatenate([qn_ref[...], qr_ref[...]], axis=1)

    def scores(off):
        kcat = jnp.concatenate([kn_ref[pl.ds(off, t), :], kr_ref[pl.ds(off, t), :]], axis=1)
        return lax.dot_general(q, kcat, _NT, preferred_element_type=F32)

    def probs(state, s):
        m_i, l_i, acc = state
        m_new = jnp.maximum(m_i, jnp.max(s, axis=-1, keepdims=True))
        alpha = jnp.exp2(m_i - m_new)
        p = jnp.exp2(s - m_new)
        l_new = alpha * l_i + jnp.sum(p, axis=-1, keepdims=True)
        return m_new, l_new, alpha, p.astype(BF16)

    def tile(ix):
        kind = jnp.where(ix < qi, 0, jnp.where(ix == qi, 1, 2))
        return pl.multiple_of(jnp.minimum(ix, qi) * t, t), kind

    def two_tiles(st_a, st_b, ix):
        off_a, kind_a = tile(ix)
        off_b, kind_b = tile(ix + 1)
        s_a = scores(off_a) + bias_ref[kind_a]
        s_b = scores(off_b) + bias_ref[kind_b]
        m_a, l_a, al_a, p_a = probs(st_a, s_a)
        m_b, l_b, al_b, p_b = probs(st_b, s_b)
        pv_a = jnp.dot(p_a, v_ref[pl.ds(off_a, t), :], preferred_element_type=F32)
        pv_b = jnp.dot(p_b, v_ref[pl.ds(off_b, t), :], preferred_element_type=F32)
        return (m_a, l_a, al_a * st_a[2] + pv_a), (m_b, l_b, al_b * st_b[2] + pv_b)

    def body(j, carry):
        st_a, st_b = two_tiles(carry[0], carry[1], 4 * j)
        return two_tiles(st_a, st_b, 4 * j + 2)

    init = (jnp.full((t, 1), NEG_BIG, F32), jnp.zeros((t, 1), F32), jnp.zeros((t, MLA_V), F32))
    (m_a, l_a, acc_a), (m_b, l_b, acc_b) = lax.fori_loop(0, qi // 4 + 1, body, (init, init))
    m_i = jnp.maximum(m_a, m_b)
    w_a = jnp.exp2(m_a - m_i)
    w_b = jnp.exp2(m_b - m_i)
    o_ref[...] = ((w_a * acc_a + w_b * acc_b) / (w_a * l_a + w_b * l_b)).astype(o_ref.dtype)


def _attention(q, kr, kv, *, nb, tp):
    m = q.shape[0]
    tq = _pick(tp, (384, 256, 128))
    nq = tp // tq
    kern = functools.partial(_attn_kernel, t=tq)
    causal = jnp.arange(tq)[None, :] <= jnp.arange(tq)[:, None]
    bias = jnp.stack([jnp.zeros((tq, tq), F32), jnp.where(causal, 0.0, NEG_BIG).astype(F32),
                      jnp.full((tq, tq), NEG_BIG, F32)])
    return pl.pallas_call(
        kern,
        grid=(nb, MLA_HEADS, nq),
        in_specs=[pl.BlockSpec((tq, LANES), lambda b, h, i: (b * nq + i, h)),
                  pl.BlockSpec((tq, LANES), lambda b, h, i: (b * nq + i, MLA_HEADS + h)),
                  pl.BlockSpec((tp, LANES), lambda b, h, i: (b, h)),
                  pl.BlockSpec((tp, LANES), lambda b, h, i: (b, 0)),
                  pl.BlockSpec((tp, LANES), lambda b, h, i: (b, MLA_HEADS + h)),
                  pl.BlockSpec((3, tq, tq), lambda b, h, i: (0, 0, 0))],
        out_specs=pl.BlockSpec((tq, LANES), lambda b, h, i: (b * nq + i, h)),
        out_shape=jax.ShapeDtypeStruct((m, MLA_WIDTH), BF16),
        compiler_params=_params(("arbitrary", "arbitrary", "arbitrary")),
        name="mla_attention",
    )(q, q, kv, kr, kv, bias)


def _merge_kernel(ya_ref, yb_ref, yc_ref, ga_ref, gb_ref, gc_ref, w_ref, o_ref):
    acc = None
    for n, (y_ref, g_ref) in enumerate(((ya_ref, ga_ref), (yb_ref, gb_ref), (yc_ref, gc_ref))):
        p = jnp.dot(y_ref[...], w_ref[n], preferred_element_type=F32)
        t = jax.nn.sigmoid(g_ref[...]) * p
        acc = t if acc is None else acc + t
    o_ref[...] = acc.astype(o_ref.dtype)


def _merge(ya, yb, yc, z, w_branch, *, tm, tn):
    m = ya.shape[0]
    d = w_branch.shape[2]
    yspec = pl.BlockSpec((tm, HG_WIDTH), lambda i, j: (i, 0))

    def gspec(n):
        return pl.BlockSpec((tm, tn), lambda i, j: (i, (Z_GATES + n * d) // tn + j))

    return pl.pallas_call(
        _merge_kernel,
        grid=(m // tm, d // tn),
        in_specs=[yspec, yspec, yspec, gspec(0), gspec(1), gspec(2),
                  pl.BlockSpec((N_BRANCH, HG_WIDTH, tn), lambda i, j: (0, 0, j))],
        out_specs=pl.BlockSpec((tm, tn), lambda i, j: (i, j)),
        out_shape=jax.ShapeDtypeStruct((m, d), BF16),
        compiler_params=_params(("arbitrary", "arbitrary")),
        name="merge",
    )(ya, yb, yc, z, z, z, w_branch)


def _matmul_res_kernel(x_ref, w_ref, r_ref, o_ref):
    o_ref[...] = r_ref[...] + jnp.dot(x_ref[...], w_ref[...], preferred_element_type=F32)


def _matmul_res(x, w, res, *, tm, tn):
    m, k = x.shape
    n = w.shape[1]
    return pl.pallas_call(
        _matmul_res_kernel,
        grid=(m // tm, n // tn),
        in_specs=[pl.BlockSpec((tm, k), lambda i, j: (i, 0)),
                  pl.BlockSpec((k, tn), lambda i, j: (0, j)),
                  pl.BlockSpec((tm, tn), lambda i, j: (i, j))],
        out_specs=pl.BlockSpec((tm, tn), lambda i, j: (i, j)),
        out_shape=jax.ShapeDtypeStruct((m, n), F32),
        compiler_params=_params(("arbitrary", "arbitrary")),
        name="matmul_res",
    )(x, w, res)


def _ffn_up_kernel(x_ref, g_ref, wg_ref, wv_ref, cwg_ref, cwv_ref, cbg_ref, cbv_ref, o_ref,
                   xn_ref, eg_ref, ev_ref, pg_ref, pv_ref, *, tm, tp, pad, nb, eps):
    i = pl.program_id(0)
    j = pl.program_id(1)

    @pl.when(j == 0)
    def _():
        xn_ref[...] = _normed_rows(x_ref[...], g_ref[...], eps, i * tm, tp, pad, nb).astype(BF16)

    @pl.when(i == 0)
    def _():
        pg_ref[j] = jnp.zeros(pg_ref.shape[1:], F32)
        pv_ref[j] = jnp.zeros(pv_ref.shape[1:], F32)

    def conv(w_ref, e_ref, p_ref, cw_ref, cb_ref):
        u = jnp.dot(xn_ref[...], w_ref[...], preferred_element_type=F32)
        e_ref[0:SUBLANES, :] = p_ref[j]
        e_ref[SUBLANES:SUBLANES + tm, :] = u
        p_ref[j] = u[tm - SUBLANES:tm, :]
        u1 = e_ref[SUBLANES - 1:SUBLANES - 1 + tm, :]
        u2 = e_ref[SUBLANES - 2:SUBLANES - 2 + tm, :]
        cw = cw_ref[...]
        return cw[0:1, :] * u2 + cw[1:2, :] * u1 + cw[2:3, :] * u + cb_ref[...]

    gate = conv(wg_ref, eg_ref, pg_ref, cwg_ref, cbg_ref)
    val = conv(wv_ref, ev_ref, pv_ref, cwv_ref, cbv_ref)
    o_ref[...] = (gate * jax.nn.sigmoid(gate) * val).astype(o_ref.dtype)


def _ffn_up(h, g, w_up, conv_w, conv_b, *, tp, pad, nb, tm, tf):
    m, d = h.shape
    dff = w_up.shape[1] // 2
    nj = dff // tf
    kern = functools.partial(_ffn_up_kernel, tm=tm, tp=tp, pad=pad, nb=nb, eps=EPS)
    cb = conv_b.reshape(1, -1).astype(F32)
    return pl.pallas_call(
        kern,
        grid=(m // tm, nj),
        in_specs=[pl.BlockSpec((tm, d), lambda i, j: (i, 0)),
                  pl.BlockSpec((1, d), lambda i, j: (0, 0)),
                  pl.BlockSpec((d, tf), lambda i, j: (0, j)),
                  pl.BlockSpec((d, tf), lambda i, j: (0, nj + j)),
                  pl.BlockSpec((CONV_W, tf), lambda i, j: (0, j)),
                  pl.BlockSpec((CONV_W, tf), lambda i, j: (0, nj + j)),
                  pl.BlockSpec((1, tf), lambda i, j: (0, j)),
                  pl.BlockSpec((1, tf), lambda i, j: (0, nj + j))],
        out_specs=pl.BlockSpec((tm, tf), lambda i, j: (i, j)),
        out_shape=jax.ShapeDtypeStruct((m, dff), BF16),
        scratch_shapes=[pltpu.VMEM((tm, d), BF16),
                        pltpu.VMEM((tm + SUBLANES, tf), F32), pltpu.VMEM((tm + SUBLANES, tf), F32),
                        pltpu.VMEM((nj, SUBLANES, tf), F32), pltpu.VMEM((nj, SUBLANES, tf), F32)],
        compiler_params=_params(("arbitrary", "arbitrary")),
        name="ffn_up",
    )(h, g.reshape(1, d).astype(F32), w_up, w_up, conv_w.astype(F32), conv_w.astype(F32), cb, cb)


def _final_norm_kernel(x_ref, g_ref, o_ref):
    x = x_ref[0]
    o_ref[0] = x * lax.rsqrt(jnp.mean(x * x, axis=-1, keepdims=True) + EPS) * g_ref[...]


def _final_norm(h3, g, *, skip, seq):
    nb, tp, d = h3.shape
    tr = LANES
    return pl.pallas_call(
        _final_norm_kernel,
        grid=(nb, seq // tr),
        in_specs=[pl.BlockSpec((1, tr, d), lambda b, i: (b, skip // tr + i, 0)),
                  pl.BlockSpec((1, d), lambda b, i: (0, 0))],
        out_specs=pl.BlockSpec((1, tr, d), lambda b, i: (b, i, 0)),
        out_shape=jax.ShapeDtypeStruct((nb, seq, d), F32),
        compiler_params=_params(("arbitrary", "arbitrary")),
        name="final_norm",
    )(h3, g.reshape(1, d).astype(F32))


def _layout_w_in(w):
    d = w.shape[0]
    o = 0
    parts = {}
    for name, width in (("hg4", 4 * HG_WIDTH), ("rkv", 3 * RW_WIDTH), ("wl", RW_DECAY_LORA), ("al", RW_AAA_LORA),
                        ("gl", RW_GATE_LORA), ("cq", MLA_Q_RANK), ("ckv", MLA_KV_RANK), ("kr", MLA_ROPE),
                        ("gates", N_BRANCH * d)):
        parts[name] = w[:, o:o + width]
        o += width
    zeros = lambda n: jnp.zeros((d, n), w.dtype)
    cols = [parts["hg4"], parts["rkv"], parts["cq"], parts["ckv"],
            parts["gl"], zeros(256 - RW_GATE_LORA), parts["wl"], parts["al"],
            parts["kr"], zeros(LANES - MLA_ROPE), parts["gates"]]
    return jnp.concatenate(cols, axis=1).astype(BF16)


def _layout_w_uq(w):
    r = w.shape[0]
    w = w.reshape(r, MLA_HEADS, MLA_NOPE + MLA_ROPE)
    nope = w[:, :, :MLA_NOPE].reshape(r, MLA_HEADS * MLA_NOPE)
    rope = jnp.pad(w[:, :, MLA_NOPE:], ((0, 0), (0, 0), (0, LANES - MLA_ROPE))).reshape(r, MLA_HEADS * LANES)
    return jnp.concatenate([nope, rope], axis=1).astype(BF16)


def _layout_w_ukv(w):
    r = w.shape[0]
    w = w.reshape(r, MLA_HEADS, MLA_NOPE + MLA_V)
    return jnp.concatenate([w[:, :, :MLA_NOPE].reshape(r, -1), w[:, :, MLA_NOPE:].reshape(r, -1)], axis=1).astype(BF16)


def _rope_tables(tp, pad):
    inv = ROPE_BASE ** (-jnp.arange(0, MLA_ROPE, 2, dtype=F32) / MLA_ROPE)
    pos = jnp.arange(tp, dtype=F32) - float(pad)
    ang = pos[:, None] * inv[None, :]
    z = jnp.zeros((tp, LANES - MLA_ROPE), F32)
    cos = jnp.concatenate([jnp.cos(ang), jnp.cos(ang), z], axis=1)
    sin = jnp.concatenate([jnp.sin(ang), jnp.sin(ang), z], axis=1)
    return cos, sin


def kernel(x, meta_tokens, norm1_g, w_in, hg_lb_logits, hg_norm_g, rw_mu, rw_w0, rw_w2, rw_a0, rw_a2, rw_g2, rw_k_k, rw_k_a, rw_r_k, rw_ln_g, rw_ln_b, mla_q_norm_g, mla_w_uq, mla_kv_norm_g, mla_w_ukv, w_branch, w_out, norm2_g, ffn_w_up, ffn_conv_w, ffn_conv_b, ffn_w_down, final_norm_g):
    nb, seq, d = x.shape
    depth = w_in.shape[0]
    t = N_META + seq
    pad = (-N_META) % LANES
    tp = pad + t
    m = nb * tp
    tm = _pick(m, (768, 384, 256, 128))
    geo = dict(tp=tp, pad=pad, nb=nb)

    meta = jnp.broadcast_to(meta_tokens[None].astype(x.dtype), (nb, N_META, d))
    h = jnp.concatenate([jnp.zeros((nb, pad, d), x.dtype), meta, x], axis=1).reshape(m, d)
    cos, sin = _rope_tables(tp, pad)
    p_lb = jax.nn.softmax(hg_lb_logits.astype(F32), axis=0)
    lower_bounds = jnp.cumsum(p_lb, axis=0) - p_lb[0]
    scale = float((MLA_NOPE + MLA_ROPE) ** -0.5 * np.log2(np.e))

    for l in range(depth):
        z = _norm_matmul(h, 0, d, norm1_g[l], _layout_w_in(w_in[l]), F32, tm=tm, tn=512, **geo)
        y_a = _hgrn2(z, lower_bounds[l], hg_norm_g[l], **geo)
        pre = _rwkv_pre(z, rw_mu[l], rw_w0[l], rw_w2[l], rw_a0[l], rw_a2[l], rw_g2[l], rw_k_k[l], rw_k_a[l],
                        tm=_pick(m, (384, 256, 128)), **geo)
        y_b = _rwkv_rec(pre, rw_r_k[l], rw_ln_g[l], rw_ln_b[l], nb=nb, tp=tp)
        q = _norm_matmul(z, Z_CQ // MLA_Q_RANK, MLA_Q_RANK, mla_q_norm_g[l], _layout_w_uq(mla_w_uq[l]), F32,
                         tm=tm, tn=512, tp=tp, pad=0, nb=nb)
        kv = _norm_matmul(z, Z_CKV // MLA_KV_RANK, MLA_KV_RANK, mla_kv_norm_g[l], _layout_w_ukv(mla_w_ukv[l]), BF16,
                          tm=tm, tn=512, tp=tp, pad=0, nb=nb)
        tr = _pick(tp, (384, 256, 128))
        q_cat = _rope(q, 0, MLA_HEADS, MLA_HEADS, cos, sin, scale, True, tp=tp, tr=tr, pad=pad)
        k_rope = _rope(z, Z_KR // LANES, 0, 1, cos, sin, 1.0, False, tp=tp, tr=tr, pad=pad)
        y_c = _attention(q_cat, k_rope, kv, nb=nb, tp=tp)
        merged = _merge(y_a, y_b, y_c, z, w_branch[l].astype(BF16), tm=tm, tn=512)
        h = _matmul_res(merged, w_out[l].astype(BF16), h, tm=tm, tn=512)
        act = _ffn_up(h, norm2_g[l], ffn_w_up[l].astype(BF16), ffn_conv_w[l], ffn_conv_b[l], tm=tm, tf=512, **geo)
        h = _matmul_res(act, ffn_w_down[l].astype(BF16), h, tm=tm, tn=512)
    return _final_norm(h.reshape(nb, tp, d), final_norm_g, skip=pad + N_META, seq=seq)
```

```python
import functools

import numpy as np
import jax
import jax.numpy as jnp
from jax import lax
from jax.experimental import pallas as pl
from jax.experimental.pallas import tpu as pltpu

F32 = jnp.float32
BF16 = jnp.bfloat16
HIGHEST = lax.Precision.HIGHEST

N_META = 16
EPS = 1e-6
NEG_BIG = -1e30
F_FLOOR = 1e-30
HG_HEADS = 8
HG_DIM = 128
HG_WIDTH = HG_HEADS * HG_DIM
RW_HEAD = 64
RW_HEADS = 16
RW_WIDTH = RW_HEADS * RW_HEAD
RW_DECAY_LORA = 64
RW_AAA_LORA = 64
RW_GATE_LORA = 160
RW_GN_EPS = 64e-5
MLA_HEADS = 8
MLA_Q_RANK = 512
MLA_KV_RANK = 512
MLA_NOPE = 128
MLA_ROPE = 64
MLA_V = 128
MLA_WIDTH = MLA_HEADS * MLA_V
ROPE_BASE = 10000.0
CONV_W = 3
N_BRANCH = 3

LANES = 128
SUBLANES = 8
VMEM_LIMIT = 56 * 1024 * 1024

Z_HQ, Z_HF, Z_HI, Z_HG = 0, 1024, 2048, 3072
Z_R, Z_K, Z_V = 4096, 5120, 6144
Z_CQ, Z_CKV = 7168, 7680
Z_GL = 8192
Z_WA = 8448
Z_KR = 8576
Z_GATES = 8704
Z_COLS = Z_GATES + N_BRANCH * 2048

RW_CHUNK = 64
HG_CHUNK = 64
HG_BASE = 8
HG_PASSES = 1


def _params(sem):
    return pltpu.CompilerParams(dimension_semantics=sem, vmem_limit_bytes=VMEM_LIMIT)


def _pick(n, cands):
    for c in cands:
        if n % c == 0:
            return c
    raise ValueError(f"no tile for {n} in {cands}")


def _block_id(idx, size):
    return lax.shift_right_logical(idx, int(np.log2(size)))


def _softplus(x):
    return jnp.maximum(x, 0.0) + jnp.log(1.0 + jnp.exp(-jnp.abs(x)))


def _row_valid(row, tp, pad, nb):
    valid = None
    for b in range(nb):
        ok = jnp.logical_or(row < b * tp, row >= b * tp + pad)
        valid = ok if valid is None else jnp.logical_and(valid, ok)
    return valid


def _normed_rows(x, g, eps, row0, tp, pad, nb):
    ms = jnp.mean(x * x, axis=-1, keepdims=True)
    y = x * lax.rsqrt(ms + eps) * g
    if pad:
        row = row0 + lax.broadcasted_iota(jnp.int32, (x.shape[0], 1), 0)
        y = jnp.where(_row_valid(row, tp, pad, nb), y, 0.0)
    return y


def _norm_matmul_kernel(x_ref, g_ref, w_ref, o_ref, xn_ref, *, tm, tp, pad, nb, eps):
    i = pl.program_id(0)

    @pl.when(pl.program_id(1) == 0)
    def _():
        xn_ref[...] = _normed_rows(x_ref[...], g_ref[...], eps, i * tm, tp, pad, nb).astype(BF16)

    o_ref[...] = jnp.dot(xn_ref[...], w_ref[...], preferred_element_type=F32).astype(o_ref.dtype)


def _norm_matmul(x, col_block, k, g, w, out_dtype, *, tp, pad, nb, tm, tn):
    m = x.shape[0]
    n = w.shape[1]
    kern = functools.partial(_norm_matmul_kernel, tm=tm, tp=tp, pad=pad, nb=nb, eps=EPS)
    return pl.pallas_call(
        kern,
        grid=(m // tm, n // tn),
        in_specs=[pl.BlockSpec((tm, k), lambda i, j: (i, col_block)),
                  pl.BlockSpec((1, k), lambda i, j: (0, 0)),
                  pl.BlockSpec((k, tn), lambda i, j: (0, j))],
        out_specs=pl.BlockSpec((tm, tn), lambda i, j: (i, j)),
        out_shape=jax.ShapeDtypeStruct((m, n), out_dtype),
        scratch_shapes=[pltpu.VMEM((tm, k), BF16)],
        compiler_params=_params(("arbitrary", "arbitrary")),
        name="norm_matmul",
    )(x, g.reshape(1, k).astype(F32), w)


def _hgrn_kernel(q_ref, f_ref, i_ref, g_ref, lb_ref, ng_ref, o_ref, st_ref, *, chunk, pad, nheads):
    c = pl.program_id(1)

    @pl.when(c == 0)
    def _():
        st_ref[...] = jnp.zeros_like(st_ref)

    L = chunk
    base = HG_BASE
    heads = range(nheads)
    hs = [slice(h * HG_DIM, (h + 1) * HG_DIM) for h in heads]
    q = q_ref[...]
    zf = f_ref[...]
    v = i_ref[...]
    lb = lb_ref[...]
    row = lax.broadcasted_iota(jnp.int32, (L, 1), 0)
    valid = c * L + row >= pad
    f = lb + (1.0 - lb) * jax.nn.sigmoid(zf)
    logf = jnp.where(valid, jnp.log(jnp.maximum(f, F_FLOOR)), 0.0)
    k = jnp.where(valid, (1.0 - lb) * jax.nn.sigmoid(-zf), 0.0)
    tri = (row >= lax.broadcasted_iota(jnp.int32, (L, L), 1)).astype(F32)
    b = _mm(tri, logf)

    r = row & (base - 1)
    o = [jnp.zeros((L, HG_DIM), F32) for _ in heads]
    for lag in range(base):
        ks, bs, vs = (k, b, v) if lag == 0 else tuple(pltpu.roll(x, lag, 0) for x in (k, b, v))
        ok = r >= lag
        term = q * ks * jnp.exp(jnp.where(ok, b - bs, 0.0))
        for h in heads:
            a = jnp.sum(term[:, hs[h]], axis=-1, keepdims=True)
            o[h] = o[h] + jnp.where(ok, a, 0.0) * vs[:, hs[h]]

    size = L
    while size > base:
        half = size // 2
        blocks = []
        for lo in range(0, L, size):
            mid, hi = lo + half, lo + size
            b_mid = b[mid - 1:mid, :]
            blocks.append((lo, mid, q[mid:hi] * jnp.exp(b[mid:hi] - b_mid), k[lo:mid] * jnp.exp(b_mid - b[lo:mid])))
        sc = [[_mm_nt(qg[:, hs[h]], kg[:, hs[h]], HG_PASSES) for h in heads] for (_, _, qg, kg) in blocks]
        pv = [[_mm(sc[bi][h], v[lo:mid, hs[h]], HG_PASSES) for h in heads] for bi, (lo, mid, _, _) in enumerate(blocks)]
        zero = jnp.zeros((half, HG_DIM), F32)
        for h in heads:
            o[h] = o[h] + jnp.concatenate([x for bi in range(len(blocks)) for x in (zero, pv[bi][h])], axis=0)
        size = half

    b_last = b[L - 1:L, :]
    q_dec = q * jnp.exp(b)
    k_end = k * jnp.exp(b_last - b)
    w_end = jnp.exp(b_last)
    st = [st_ref[h] for h in heads]
    o_st = [_mm_nt(q_dec[:, hs[h]], st[h], HG_PASSES) for h in heads]
    upd = [_mm(v[:, hs[h]].T, k_end[:, hs[h]], HG_PASSES) for h in heads]
    for h in heads:
        st_ref[h] = st[h] * w_end[:, hs[h]] + upd[h]

    zg = g_ref[...]
    gate = zg * jax.nn.sigmoid(zg)
    ng = ng_ref[...]
    for h in heads:
        y = o[h] + o_st[h]
        y = y * lax.rsqrt(jnp.mean(y * y, axis=-1, keepdims=True) + EPS) * ng[:, hs[h]]
        o_ref[:, hs[h]] = (y * gate[:, hs[h]]).astype(o_ref.dtype)


def _hgrn2(z, lb, norm_g, *, nb, tp, pad):
    m = z.shape[0]
    nc = tp // HG_CHUNK
    kern = functools.partial(_hgrn_kernel, chunk=HG_CHUNK, pad=pad, nheads=HG_HEADS)

    def zspec(off):
        return pl.BlockSpec((HG_CHUNK, HG_WIDTH), lambda b, c: (b * nc + c, off // HG_WIDTH))

    pspec = pl.BlockSpec((1, HG_WIDTH), lambda b, c: (0, 0))
    return pl.pallas_call(
        kern,
        grid=(nb, nc),
        in_specs=[zspec(Z_HQ), zspec(Z_HF), zspec(Z_HI), zspec(Z_HG), pspec, pspec],
        out_specs=pl.BlockSpec((HG_CHUNK, HG_WIDTH), lambda b, c: (b * nc + c, 0)),
        out_shape=jax.ShapeDtypeStruct((m, HG_WIDTH), BF16),
        scratch_shapes=[pltpu.VMEM((HG_HEADS, HG_DIM, HG_DIM), F32)],
        compiler_params=_params(("arbitrary", "arbitrary")),
        name="hgrn2",
    )(z, z, z, z, lb.reshape(1, HG_WIDTH).astype(F32), norm_g.reshape(1, HG_WIDTH).astype(F32))


def _rwkv_pre_kernel(zr_ref, zk_ref, zv_ref, zg_ref, zwa_ref,
                     mr_ref, mk_ref, mv_ref, mg_ref, mwa_ref,
                     w0_ref, w2_ref, a0_ref, a2_ref, g2_ref, kk_ref, ka_ref,
                     r_out, lw_out, k_out, v_out, kk_out, a_out, g_out,
                     er_ref, ek_ref, ev_ref, eg_ref, ewa_ref, *, tm, tp, pad, nb):
    i = pl.program_id(0)

    def shifted(z_ref, e_ref, mu_ref):
        @pl.when(i == 0)
        def _():
            e_ref[0:SUBLANES, :] = jnp.zeros((SUBLANES, e_ref.shape[1]), F32)

        x = z_ref[...]
        e_ref[SUBLANES:SUBLANES + tm, :] = x
        prev = e_ref[SUBLANES - 1:SUBLANES - 1 + tm, :]
        e_ref[0:SUBLANES, :] = x[tm - SUBLANES:tm, :]
        return x + mu_ref[...] * (prev - x)

    xr = shifted(zr_ref, er_ref, mr_ref)
    xk = shifted(zk_ref, ek_ref, mk_ref)
    xv = shifted(zv_ref, ev_ref, mv_ref)
    xg = shifted(zg_ref, eg_ref, mg_ref)
    xwa = shifted(zwa_ref, ewa_ref, mwa_ref)

    row = i * tm + lax.broadcasted_iota(jnp.int32, (tm, 1), 0)
    valid = _row_valid(row, tp, pad, nb)

    w = w0_ref[...] + jnp.dot(jnp.tanh(xwa), w2_ref[...], precision=HIGHEST, preferred_element_type=F32)
    w = -_softplus(-w) - 0.5
    lw_out[...] = -jnp.exp(w)
    a = jax.nn.sigmoid(a0_ref[...] + jnp.dot(xwa, a2_ref[...], precision=HIGHEST, preferred_element_type=F32))
    a_out[...] = a.astype(a_out.dtype)
    g_out[...] = jnp.dot(jax.nn.sigmoid(xg).astype(BF16), g2_ref[...],
                         preferred_element_type=F32).astype(g_out.dtype)
    r_out[...] = xr.astype(r_out.dtype)
    v_out[...] = jnp.where(valid, xv, 0.0).astype(v_out.dtype)
    k_out[...] = jnp.where(valid, xk * (1.0 + (a - 1.0) * ka_ref[...]), 0.0).astype(k_out.dtype)
    kk_out[...] = jnp.where(valid, xk * kk_ref[...], 0.0).astype(kk_out.dtype)


def _rwkv_pre(z, mu, w0, w2, a0, a2, g2, k_k, k_a, *, nb, tp, pad, tm):
    m = z.shape[0]
    mu_r, mu_k, mu_v = (mu[s:s + RW_WIDTH].reshape(1, RW_WIDTH) for s in (0, RW_WIDTH, 2 * RW_WIDTH))
    o = 3 * RW_WIDTH
    mu_wa = mu[o:o + 128].reshape(1, 128)
    mu_g = jnp.pad(mu[o + 128:o + 128 + RW_GATE_LORA], (0, 256 - RW_GATE_LORA)).reshape(1, 256)
    w2p = jnp.concatenate([w2, jnp.zeros_like(a2)], axis=0)
    a2p = jnp.concatenate([jnp.zeros_like(w2), a2], axis=0)
    g2p = jnp.pad(g2, ((0, 256 - RW_GATE_LORA), (0, 0)))
    kern = functools.partial(_rwkv_pre_kernel, tm=tm, tp=tp, pad=pad, nb=nb)

    def zspec(off, width):
        return pl.BlockSpec((tm, width), lambda i: (i, off // width))

    def full(shape):
        return pl.BlockSpec(shape, lambda i: (0,) * len(shape))

    vec = full((1, RW_WIDTH))
    wide = pl.BlockSpec((tm, RW_WIDTH), lambda i: (i, 0))
    outs = pl.pallas_call(
        kern,
        grid=(m // tm,),
        in_specs=[zspec(Z_R, RW_WIDTH), zspec(Z_K, RW_WIDTH), zspec(Z_V, RW_WIDTH), zspec(Z_GL, 256), zspec(Z_WA, 128),
                  vec, vec, vec, full((1, 256)), full((1, 128)),
                  vec, full((128, RW_WIDTH)), vec, full((128, RW_WIDTH)), full((256, RW_WIDTH)), vec, vec],
        out_specs=[wide] * 7,
        out_shape=[jax.ShapeDtypeStruct((m, RW_WIDTH), F32 if n == 1 else BF16) for n in range(7)],
        scratch_shapes=[pltpu.VMEM((tm + SUBLANES, RW_WIDTH), F32)] * 3
        + [pltpu.VMEM((tm + SUBLANES, 256), F32), pltpu.VMEM((tm + SUBLANES, 128), F32)],
        compiler_params=_params(("arbitrary",)),
        name="rwkv_pre",
    )(z, z, z, z, z, mu_r, mu_k, mu_v, mu_g, mu_wa,
      w0.reshape(1, -1), w2p, a0.reshape(1, -1), a2p, g2p.astype(BF16), k_k.reshape(1, -1), k_a.reshape(1, -1))
    return outs


_NN = (((1,), (0,)), ((), ()))
_NT = (((1,), (1,)), ((), ()))


def _split2(a):
    hi = a.astype(BF16)
    return hi, (a - hi.astype(F32)).astype(BF16)


def _dot(a, b, dims, passes):
    if passes == 6:
        return lax.dot_general(a, b, dims, precision=HIGHEST, preferred_element_type=F32)
    if passes == 1:
        return lax.dot_general(a.astype(BF16), b.astype(BF16), dims, preferred_element_type=F32)
    ah, al = _split2(a)
    bh, bl = _split2(b)
    out = lax.dot_general(ah, bh, dims, preferred_element_type=F32)
    out = out + lax.dot_general(ah, bl, dims, preferred_element_type=F32)
    return out + lax.dot_general(al, bh, dims, preferred_element_type=F32)


def _mm(a, b, passes=6):
    return _dot(a, b, _NN, passes)


def _mm_nt(a, b, passes=6):
    return _dot(a, b, _NT, passes)


def _bf(x):
    return x.astype(BF16)


def _bmm(a, b):
    return lax.dot_general(a, b, _NN, preferred_element_type=F32)


def _bmm_nt(a, b):
    return lax.dot_general(a, b, _NT, preferred_element_type=F32)


def _geometric_sums(xs, eye, steps):
    n = xs[0].shape[0]
    s = [eye + x for x in xs]
    if steps == 1:
        return s
    p = [_bmm(b, b) for b in map(_bf, xs)]
    for k in range(1, steps):
        pb = [_bf(pp) for pp in p]
        if k == steps - 1:
            s = [ss + _bmm(b, _bf(ss)) for ss, b in zip(s, pb)]
        else:
            both = [_bmm(b, jnp.concatenate([b, _bf(ss)], axis=1)) for b, ss in zip(pb, s)]
            p = [r[:, 0:n] for r in both]
            s = [ss + r[:, n:2 * n] for ss, r in zip(s, both)]
    return s


def _unit_lower_inverse(n_mats, blk):
    n = n_mats[0].shape[0]
    ri = lax.broadcasted_iota(jnp.int32, (n, n), 0)
    ci = lax.broadcasted_iota(jnp.int32, (n, n), 1)
    eye = (ri == ci).astype(F32)
    diag_blk = _block_id(ri, blk) == _block_id(ci, blk)
    d_inv = _geometric_sums([jnp.where(diag_blk, -m, 0.0) for m in n_mats], eye, int(np.log2(blk)))
    d_inv_b = [_bf(d) for d in d_inv]
    m2 = [-_bmm(d, _bf(jnp.where(diag_blk, 0.0, m))) for d, m in zip(d_inv_b, n_mats)]
    t = _geometric_sums(m2, eye, int(np.log2(RW_CHUNK // blk)))
    return [_bmm(_bf(tt), d) for tt, d in zip(t, d_inv_b)]


def _rwkv_rec_kernel(r_ref, lw_ref, k_ref, v_ref, kk_ref, a_ref, g_ref, rk_ref, lng_ref, lnb_ref,
                     o_ref, st_ref, *, chunk, npair):
    @pl.when(pl.program_id(1) == 0)
    def _():
        st_ref[...] = jnp.zeros_like(st_ref)

    L = chunk
    n2 = 2 * L
    lane = lax.broadcasted_iota(jnp.int32, (1, LANES), 1)
    m0 = (lane < RW_HEAD).astype(F32)
    m1 = 1.0 - m0
    tri = (lax.broadcasted_iota(jnp.int32, (L, L), 0) >= lax.broadcasted_iota(jnp.int32, (L, L), 1)).astype(F32)
    ri = lax.broadcasted_iota(jnp.int32, (n2, n2), 0)
    ci = lax.broadcasted_iota(jnp.int32, (n2, n2), 1)
    strict = ri > ci
    incl = ri >= ci
    cum_all = _mm(tri, lw_ref[...])

    def stack(x):
        return jnp.concatenate([x * m0, x * m1], axis=0)

    def head_mean(x):
        s0 = jnp.sum(x * m0, axis=-1, keepdims=True)
        s1 = jnp.sum(x * m1, axis=-1, keepdims=True)
        return (s0 * m0 + s1 * m1) * (1.0 / RW_HEAD)

    pairs = range(npair)
    sls = [slice(p * LANES, (p + 1) * LANES) for p in pairs]
    ops = []
    for sl in sls:
        r = r_ref[:, sl].astype(F32)
        lw = lw_ref[:, sl]
        k = k_ref[:, sl].astype(F32)
        kk = kk_ref[:, sl].astype(F32)
        kn = kk / jnp.maximum(jnp.sqrt(head_mean(kk * kk) * RW_HEAD + 1e-24), 1e-12)
        cum = cum_all[:, sl]
        c_last = cum[L - 1:L, :]
        e_neg = jnp.exp(-cum)
        e_end = jnp.exp(c_last - cum)
        beta = kn * a_ref[:, sl].astype(F32)
        v_s = stack(v_ref[:, sl].astype(F32))
        ops.append(dict(
            al_t=_bf(stack(kn * jnp.exp(cum - lw))),
            r_t=_bf(stack(r * jnp.exp(cum))),
            be_h=_bf(stack(beta * e_neg)),
            k_h=_bf(stack(k * e_neg)),
            be_e=_bf(stack(beta * e_end)),
            k_e=_bf(stack(k * e_end)),
            v_s=v_s, v_b=_bf(v_s),
            w_end=jnp.exp(c_last)))

    sc = [_bmm_nt(jnp.concatenate([o["al_t"], o["r_t"]], axis=0), jnp.concatenate([o["be_h"], o["k_h"]], axis=0))
          for o in ops]
    a_ab = [jnp.where(strict, s[0:n2, 0:n2], 0.0) for s in sc]
    a_ak = [_bf(jnp.where(strict, s[0:n2, n2:2 * n2], 0.0)) for s in sc]
    a_rb = [_bf(jnp.where(incl, s[n2:2 * n2, 0:n2], 0.0)) for s in sc]
    a_rk = [_bf(jnp.where(incl, s[n2:2 * n2, n2:2 * n2], 0.0)) for s in sc]
    gy = [_bmm(jnp.concatenate([ak, rk], axis=0), o["v_b"]) for ak, rk, o in zip(a_ak, a_rk, ops)]
    t_inv = _unit_lower_inverse(a_ab, 16)
    hats = [_bmm(_bf(t), jnp.concatenate([o["al_t"], _bf(g[0:n2])], axis=1)) for t, o, g in zip(t_inv, ops, gy)]

    st = [st_ref[p] for p in pairs]
    from_st = [_bmm_nt(jnp.concatenate([_bf(h[:, 0:LANES]), o["r_t"]], axis=0), _bf(s))
               for h, o, s in zip(hats, ops, st)]
    u = [-(f[0:n2] + h[:, LANES:2 * LANES]) for f, h in zip(from_st, hats)]
    y_u = [_bmm(a, _bf(uu)) for a, uu in zip(a_rb, u)]
    upd = [_bmm(_bf(jnp.concatenate([uu, o["v_s"]], axis=0).T), jnp.concatenate([o["be_e"], o["k_e"]], axis=0))
           for uu, o in zip(u, ops)]
    for p in pairs:
        st_ref[p] = st[p] * ops[p]["w_end"] + upd[p]
    y_st = [f[n2:2 * n2] for f in from_st]
    y_loc = [g[n2:2 * n2] for g in gy]

    for p, sl in enumerate(sls):
        y = y_st[p] + y_u[p] + y_loc[p]
        y = y[0:L] + y[L:n2]
        r = r_ref[:, sl].astype(F32)
        k = k_ref[:, sl].astype(F32)
        v = v_ref[:, sl].astype(F32)
        mu = head_mean(y)
        yc = y - mu
        var = head_mean(yc * yc)
        yn = yc * lax.rsqrt(var + RW_GN_EPS) * lng_ref[:, sl] + lnb_ref[:, sl]
        bonus = head_mean(r * k * rk_ref[:, sl]) * RW_HEAD
        o_ref[:, sl] = ((yn + bonus * v) * g_ref[:, sl]).astype(o_ref.dtype)


def _rwkv_rec(pre, r_k, ln_g, ln_b, *, nb, tp):
    r, lw, k, v, kn, a, g = pre
    m = r.shape[0]
    L = RW_CHUNK
    nc = tp // L
    npair = RW_WIDTH // LANES
    kern = functools.partial(_rwkv_rec_kernel, chunk=L, npair=npair)
    blk = pl.BlockSpec((L, RW_WIDTH), lambda b, c: (b * nc + c, 0))
    pspec = pl.BlockSpec((1, RW_WIDTH), lambda b, c: (0, 0))
    return pl.pallas_call(
        kern,
        grid=(nb, nc),
        in_specs=[blk] * 7 + [pspec] * 3,
        out_specs=blk,
        out_shape=jax.ShapeDtypeStruct((m, RW_WIDTH), BF16),
        scratch_shapes=[pltpu.VMEM((npair, LANES, LANES), F32)],
        compiler_params=_params(("arbitrary", "arbitrary")),
        name="rwkv_rec",
    )(r, lw, k, v, kn, a, g, r_k.reshape(1, -1).astype(F32), ln_g.reshape(1, -1).astype(F32),
      ln_b.reshape(1, -1).astype(F32))


def _rope_kernel(x_ref, cos_ref, sin_ref, o_ref, *, n_plain, n_rope, scale, is_query, tr, nrb, pad):
    lane = lax.broadcasted_iota(jnp.int32, (1, LANES), 1)
    cos = cos_ref[...]
    sin = sin_ref[...]
    if is_query:
        bias = jnp.ones((tr, 1), F32)
    else:
        pos = (pl.program_id(0) % nrb) * tr + lax.broadcasted_iota(jnp.int32, (tr, 1), 0)
        bias = jnp.where(pos < pad, NEG_BIG, 0.0)
    if n_plain:
        o_ref[:, 0:n_plain * LANES] = (x_ref[:, 0:n_plain * LANES] * scale).astype(o_ref.dtype)
    for h in range(n_plain, n_plain + n_rope):
        x = x_ref[:, h * LANES:(h + 1) * LANES]
        rot = jnp.where(lane < MLA_ROPE // 2, -pltpu.roll(x, LANES - MLA_ROPE // 2, 1),
                        pltpu.roll(x, MLA_ROPE // 2, 1))
        y = (x * cos + rot * sin) * scale
        o_ref[:, h * LANES:(h + 1) * LANES] = jnp.where(lane == MLA_ROPE, bias, y).astype(o_ref.dtype)


def _rope(x, col_block, n_plain, n_rope, cos, sin, scale, is_query, *, tp, tr, pad):
    m = x.shape[0]
    nrb = tp // tr
    width = (n_plain + n_rope) * LANES
    kern = functools.partial(_rope_kernel, n_plain=n_plain, n_rope=n_rope, scale=scale, is_query=is_query,
                             tr=tr, nrb=nrb, pad=pad)
    tab = pl.BlockSpec((tr, LANES), lambda i: (i % nrb, 0))
    return pl.pallas_call(
        kern,
        grid=(m // tr,),
        in_specs=[pl.BlockSpec((tr, width), lambda i: (i, col_block)), tab, tab],
        out_specs=pl.BlockSpec((tr, width), lambda i: (i, 0)),
        out_shape=jax.ShapeDtypeStruct((m, width), BF16),
        compiler_params=_params(("arbitrary",)),
        name="rope",
    )(x, cos, sin)


def _attn_kernel(qn_ref, qr_ref, kn_ref, kr_ref, v_ref, bias_ref, o_ref, *, t, nq):
    qi = pl.program_id(2)
    q = jnp.concatenate([qn_ref[...], qr_ref[...]], axis=1)

    def scores(off):
        kcat = jnp.concatenate([kn_ref[pl.ds(off, 2 * t), :], kr_ref[pl.ds(off, 2 * t), :]], axis=1)
        return lax.dot_general(q, kcat, _NT, preferred_element_type=F32)

    def probs(state, s):
        m_i, l_i, acc = state
        m_new = jnp.maximum(m_i, jnp.max(s, axis=-1, keepdims=True))
        alpha = jnp.exp2(m_i - m_new)
        p = jnp.exp2(s - m_new)
        l_new = alpha * l_i + jnp.sum(p, axis=-1, keepdims=True)
        return m_new, l_new, alpha, p.astype(BF16)

    def tile_pair(ix):
        first = jnp.minimum(jnp.minimum(ix, qi), nq - 2)

        def kind(x):
            return jnp.where(x < ix, 2, jnp.where(x < qi, 0, jnp.where(x == qi, 1, 2)))

        bias = jnp.concatenate([bias_ref[kind(first)], bias_ref[kind(first + 1)]], axis=1)
        return pl.multiple_of(first * t, t), bias

    def body(j, carry):
        st_a, st_b = carry
        off_a, bias_a = tile_pair(4 * j)
        off_b, bias_b = tile_pair(4 * j + 2)
        s_a = scores(off_a) + bias_a
        s_b = scores(off_b) + bias_b
        m_a, l_a, al_a, p_a = probs(st_a, s_a)
        m_b, l_b, al_b, p_b = probs(st_b, s_b)
        pv_a = jnp.dot(p_a, v_ref[pl.ds(off_a, 2 * t), :], preferred_element_type=F32)
        pv_b = jnp.dot(p_b, v_ref[pl.ds(off_b, 2 * t), :], preferred_element_type=F32)
        return (m_a, l_a, al_a * st_a[2] + pv_a), (m_b, l_b, al_b * st_b[2] + pv_b)

    init = (jnp.full((t, 1), NEG_BIG, F32), jnp.zeros((t, 1), F32), jnp.zeros((t, MLA_V), F32))
    (m_a, l_a, acc_a), (m_b, l_b, acc_b) = lax.fori_loop(0, qi // 4 + 1, body, (init, init))
    m_i = jnp.maximum(m_a, m_b)
    w_a = jnp.exp2(m_a - m_i)
    w_b = jnp.exp2(m_b - m_i)
    o_ref[...] = ((w_a * acc_a + w_b * acc_b) / (w_a * l_a + w_b * l_b)).astype(o_ref.dtype)


def _attention(q, kr, kv, *, nb, tp):
    m = q.shape[0]
    tq = _pick(tp, (384, 256, 128))
    nq = tp // tq
    kern = functools.partial(_attn_kernel, t=tq, nq=nq)
    causal = jnp.arange(tq)[None, :] <= jnp.arange(tq)[:, None]
    bias = jnp.stack([jnp.zeros((tq, tq), F32), jnp.where(causal, 0.0, NEG_BIG).astype(F32),
                      jnp.full((tq, tq), NEG_BIG, F32)])
    return pl.pallas_call(
        kern,
        grid=(nb, MLA_HEADS, nq),
        in_specs=[pl.BlockSpec((tq, LANES), lambda b, h, i: (b * nq + i, h)),
                  pl.BlockSpec((tq, LANES), lambda b, h, i: (b * nq + i, MLA_HEADS + h)),
                  pl.BlockSpec((tp, LANES), lambda b, h, i: (b, h)),
                  pl.BlockSpec((tp, LANES), lambda b, h, i: (b, 0)),
                  pl.BlockSpec((tp, LANES), lambda b, h, i: (b, MLA_HEADS + h)),
                  pl.BlockSpec((3, tq, tq), lambda b, h, i: (0, 0, 0))],
        out_specs=pl.BlockSpec((tq, LANES), lambda b, h, i: (b * nq + i, h)),
        out_shape=jax.ShapeDtypeStruct((m, MLA_WIDTH), BF16),
        compiler_params=_params(("arbitrary", "arbitrary", "arbitrary")),
        name="mla_attention",
    )(q, q, kv, kr, kv, bias)


def _merge_kernel(ya_ref, yb_ref, yc_ref, ga_ref, gb_ref, gc_ref, w_ref, o_ref):
    acc = None
    for n, (y_ref, g_ref) in enumerate(((ya_ref, ga_ref), (yb_ref, gb_ref), (yc_ref, gc_ref))):
        p = jnp.dot(y_ref[...], w_ref[n], preferred_element_type=F32)
        t = jax.nn.sigmoid(g_ref[...]) * p
        acc = t if acc is None else acc + t
    o_ref[...] = acc.astype(o_ref.dtype)


def _merge(ya, yb, yc, z, w_branch, *, tm, tn):
    m = ya.shape[0]
    d = w_branch.shape[2]
    yspec = pl.BlockSpec((tm, HG_WIDTH), lambda i, j: (i, 0))

    def gspec(n):
        return pl.BlockSpec((tm, tn), lambda i, j: (i, (Z_GATES + n * d) // tn + j))

    return pl.pallas_call(
        _merge_kernel,
        grid=(m // tm, d // tn),
        in_specs=[yspec, yspec, yspec, gspec(0), gspec(1), gspec(2),
                  pl.BlockSpec((N_BRANCH, HG_WIDTH, tn), lambda i, j: (0, 0, j))],
        out_specs=pl.BlockSpec((tm, tn), lambda i, j: (i, j)),
        out_shape=jax.ShapeDtypeStruct((m, d), BF16),
        compiler_params=_params(("arbitrary", "arbitrary")),
        name="merge",
    )(ya, yb, yc, z, z, z, w_branch)


def _matmul_res_kernel(x_ref, w_ref, r_ref, o_ref):
    o_ref[...] = r_ref[...] + jnp.dot(x_ref[...], w_ref[...], preferred_element_type=F32)


def _matmul_res(x, w, res, *, tm, tn):
    m, k = x.shape
    n = w.shape[1]
    return pl.pallas_call(
        _matmul_res_kernel,
        grid=(m // tm, n // tn),
        in_specs=[pl.BlockSpec((tm, k), lambda i, j: (i, 0)),
                  pl.BlockSpec((k, tn), lambda i, j: (0, j)),
                  pl.BlockSpec((tm, tn), lambda i, j: (i, j))],
        out_specs=pl.BlockSpec((tm, tn), lambda i, j: (i, j)),
        out_shape=jax.ShapeDtypeStruct((m, n), F32),
        compiler_params=_params(("arbitrary", "arbitrary")),
        name="matmul_res",
    )(x, w, res)


def _ffn_up_kernel(x_ref, g_ref, wg_ref, wv_ref, cwg_ref, cwv_ref, cbg_ref, cbv_ref, o_ref,
                   xn_ref, eg_ref, ev_ref, pg_ref, pv_ref, *, tm, tp, pad, nb, eps):
    i = pl.program_id(0)
    j = pl.program_id(1)

    @pl.when(j == 0)
    def _():
        xn_ref[...] = _normed_rows(x_ref[...], g_ref[...], eps, i * tm, tp, pad, nb).astype(BF16)

    @pl.when(i == 0)
    def _():
        pg_ref[j] = jnp.zeros(pg_ref.shape[1:], F32)
        pv_ref[j] = jnp.zeros(pv_ref.shape[1:], F32)

    def conv(w_ref, e_ref, p_ref, cw_ref, cb_ref):
        u = jnp.dot(xn_ref[...], w_ref[...], preferred_element_type=F32)
        e_ref[0:SUBLANES, :] = p_ref[j]
        e_ref[SUBLANES:SUBLANES + tm, :] = u
        p_ref[j] = u[tm - SUBLANES:tm, :]
        u1 = e_ref[SUBLANES - 1:SUBLANES - 1 + tm, :]
        u2 = e_ref[SUBLANES - 2:SUBLANES - 2 + tm, :]
        cw = cw_ref[...]
        return cw[0:1, :] * u2 + cw[1:2, :] * u1 + cw[2:3, :] * u + cb_ref[...]

    gate = conv(wg_ref, eg_ref, pg_ref, cwg_ref, cbg_ref)
    val = conv(wv_ref, ev_ref, pv_ref, cwv_ref, cbv_ref)
    o_ref[...] = (gate * jax.nn.sigmoid(gate) * val).astype(o_ref.dtype)


def _ffn_up(h, g, w_up, conv_w, conv_b, *, tp, pad, nb, tm, tf):
    m, d = h.shape
    dff = w_up.shape[1] // 2
    nj = dff // tf
    kern = functools.partial(_ffn_up_kernel, tm=tm, tp=tp, pad=pad, nb=nb, eps=EPS)
    cb = conv_b.reshape(1, -1).astype(F32)
    return pl.pallas_call(
        kern,
        grid=(m // tm, nj),
        in_specs=[pl.BlockSpec((tm, d), lambda i, j: (i, 0)),
                  pl.BlockSpec((1, d), lambda i, j: (0, 0)),
                  pl.BlockSpec((d, tf), lambda i, j: (0, j)),
                  pl.BlockSpec((d, tf), lambda i, j: (0, nj + j)),
                  pl.BlockSpec((CONV_W, tf), lambda i, j: (0, j)),
                  pl.BlockSpec((CONV_W, tf), lambda i, j: (0, nj + j)),
                  pl.BlockSpec((1, tf), lambda i, j: (0, j)),
                  pl.BlockSpec((1, tf), lambda i, j: (0, nj + j))],
        out_specs=pl.BlockSpec((tm, tf), lambda i, j: (i, j)),
        out_shape=jax.ShapeDtypeStruct((m, dff), BF16),
        scratch_shapes=[pltpu.VMEM((tm, d), BF16),
                        pltpu.VMEM((tm + SUBLANES, tf), F32), pltpu.VMEM((tm + SUBLANES, tf), F32),
                        pltpu.VMEM((nj, SUBLANES, tf), F32), pltpu.VMEM((nj, SUBLANES, tf), F32)],
        compiler_params=_params(("arbitrary", "arbitrary")),
        name="ffn_up",
    )(h, g.reshape(1, d).astype(F32), w_up, w_up, conv_w.astype(F32), conv_w.astype(F32), cb, cb)


def _final_norm_kernel(x_ref, g_ref, o_ref):
    x = x_ref[0]
    o_ref[0] = x * lax.rsqrt(jnp.mean(x * x, axis=-1, keepdims=True) + EPS) * g_ref[...]


def _final_norm(h3, g, *, skip, seq):
    nb, tp, d = h3.shape
    tr = LANES
    return pl.pallas_call(
        _final_norm_kernel,
        grid=(nb, seq // tr),
        in_specs=[pl.BlockSpec((1, tr, d), lambda b, i: (b, skip // tr + i, 0)),
                  pl.BlockSpec((1, d), lambda b, i: (0, 0))],
        out_specs=pl.BlockSpec((1, tr, d), lambda b, i: (b, i, 0)),
        out_shape=jax.ShapeDtypeStruct((nb, seq, d), F32),
        compiler_params=_params(("arbitrary", "arbitrary")),
        name="final_norm",
    )(h3, g.reshape(1, d).astype(F32))


def _layout_w_in(w):
    d = w.shape[0]
    o = 0
    parts = {}
    for name, width in (("hg4", 4 * HG_WIDTH), ("rkv", 3 * RW_WIDTH), ("wl", RW_DECAY_LORA), ("al", RW_AAA_LORA),
                        ("gl", RW_GATE_LORA), ("cq", MLA_Q_RANK), ("ckv", MLA_KV_RANK), ("kr", MLA_ROPE),
                        ("gates", N_BRANCH * d)):
        parts[name] = w[:, o:o + width]
        o += width
    zeros = lambda n: jnp.zeros((d, n), w.dtype)
    cols = [parts["hg4"], parts["rkv"], parts["cq"], parts["ckv"],
            parts["gl"], zeros(256 - RW_GATE_LORA), parts["wl"], parts["al"],
            parts["kr"], zeros(LANES - MLA_ROPE), parts["gates"]]
    return jnp.concatenate(cols, axis=1).astype(BF16)


def _layout_w_uq(w):
    r = w.shape[0]
    w = w.reshape(r, MLA_HEADS, MLA_NOPE + MLA_ROPE)
    nope = w[:, :, :MLA_NOPE].reshape(r, MLA_HEADS * MLA_NOPE)
    rope = jnp.pad(w[:, :, MLA_NOPE:], ((0, 0), (0, 0), (0, LANES - MLA_ROPE))).reshape(r, MLA_HEADS * LANES)
    return jnp.concatenate([nope, rope], axis=1).astype(BF16)


def _layout_w_ukv(w):
    r = w.shape[0]
    w = w.reshape(r, MLA_HEADS, MLA_NOPE + MLA_V)
    return jnp.concatenate([w[:, :, :MLA_NOPE].reshape(r, -1), w[:, :, MLA_NOPE:].reshape(r, -1)], axis=1).astype(BF16)


def _rope_tables(tp, pad):
    inv = ROPE_BASE ** (-jnp.arange(0, MLA_ROPE, 2, dtype=F32) / MLA_ROPE)
    pos = jnp.arange(tp, dtype=F32) - float(pad)
    ang = pos[:, None] * inv[None, :]
    z = jnp.zeros((tp, LANES - MLA_ROPE), F32)
    cos = jnp.concatenate([jnp.cos(ang), jnp.cos(ang), z], axis=1)
    sin = jnp.concatenate([jnp.sin(ang), jnp.sin(ang), z], axis=1)
    return cos, sin


def kernel(x, meta_tokens, norm1_g, w_in, hg_lb_logits, hg_norm_g, rw_mu, rw_w0, rw_w2, rw_a0, rw_a2, rw_g2, rw_k_k, rw_k_a, rw_r_k, rw_ln_g, rw_ln_b, mla_q_norm_g, mla_w_uq, mla_kv_norm_g, mla_w_ukv, w_branch, w_out, norm2_g, ffn_w_up, ffn_conv_w, ffn_conv_b, ffn_w_down, final_norm_g):
    nb, seq, d = x.shape
    depth = w_in.shape[0]
    t = N_META + seq
    pad = (-N_META) % LANES
    tp = pad + t
    m = nb * tp
    tm = _pick(m, (768, 384, 256, 128))
    geo = dict(tp=tp, pad=pad, nb=nb)

    meta = jnp.broadcast_to(meta_tokens[None].astype(x.dtype), (nb, N_META, d))
    h = jnp.concatenate([jnp.zeros((nb, pad, d), x.dtype), meta, x], axis=1).reshape(m, d)
    cos, sin = _rope_tables(tp, pad)
    p_lb = jax.nn.softmax(hg_lb_logits.astype(F32), axis=0)
    lower_bounds = jnp.cumsum(p_lb, axis=0) - p_lb[0]
    scale = float((MLA_NOPE + MLA_ROPE) ** -0.5 * np.log2(np.e))

    for l in range(depth):
        z = _norm_matmul(h, 0, d, norm1_g[l], _layout_w_in(w_in[l]), F32, tm=tm, tn=512, **geo)
        y_a = _hgrn2(z, lower_bounds[l], hg_norm_g[l], **geo)
        pre = _rwkv_pre(z, rw_mu[l], rw_w0[l], rw_w2[l], rw_a0[l], rw_a2[l], rw_g2[l], rw_k_k[l], rw_k_a[l],
                        tm=_pick(m, (384, 256, 128)), **geo)
        y_b = _rwkv_rec(pre, rw_r_k[l], rw_ln_g[l], rw_ln_b[l], nb=nb, tp=tp)
        q = _norm_matmul(z, Z_CQ // MLA_Q_RANK, MLA_Q_RANK, mla_q_norm_g[l], _layout_w_uq(mla_w_uq[l]), F32,
                         tm=tm, tn=512, tp=tp, pad=0, nb=nb)
        kv = _norm_matmul(z, Z_CKV // MLA_KV_RANK, MLA_KV_RANK, mla_kv_norm_g[l], _layout_w_ukv(mla_w_ukv[l]), BF16,
                          tm=tm, tn=512, tp=tp, pad=0, nb=nb)
        tr = _pick(tp, (384, 256, 128))
        q_cat = _rope(q, 0, MLA_HEADS, MLA_HEADS, cos, sin, scale, True, tp=tp, tr=tr, pad=pad)
        k_rope = _rope(z, Z_KR // LANES, 0, 1, cos, sin, 1.0, False, tp=tp, tr=tr, pad=pad)
        y_c = _attention(q_cat, k_rope, kv, nb=nb, tp=tp)
        merged = _merge(y_a, y_b, y_c, z, w_branch[l].astype(BF16), tm=tm, tn=512)
        h = _matmul_res(merged, w_out[l].astype(BF16), h, tm=tm, tn=512)
        act = _ffn_up(h, norm2_g[l], ffn_w_up[l].astype(BF16), ffn_conv_w[l], ffn_conv_b[l], tm=tm, tf=512, **geo)
        h = _matmul_res(act, ffn_w_down[l].astype(BF16), h, tm=tm, tn=512)
    return _final_norm(h.reshape(nb, tp, d), final_norm_g, skip=pad + N_META, seq=seq)
```

```python
import functools

import numpy as np
import jax
import jax.numpy as jnp
from jax import lax
from jax.experimental import pallas as pl
from jax.experimental.pallas import tpu as pltpu

F32 = jnp.float32
BF16 = jnp.bfloat16
HIGHEST = lax.Precision.HIGHEST

N_META = 16
EPS = 1e-6
NEG_BIG = -1e30
F_FLOOR = 1e-30
HG_HEADS = 8
HG_DIM = 128
HG_WIDTH = HG_HEADS * HG_DIM
RW_HEAD = 64
RW_HEADS = 16
RW_WIDTH = RW_HEADS * RW_HEAD
RW_DECAY_LORA = 64
RW_AAA_LORA = 64
RW_GATE_LORA = 160
RW_GN_EPS = 64e-5
MLA_HEADS = 8
MLA_Q_RANK = 512
MLA_KV_RANK = 512
MLA_NOPE = 128
MLA_ROPE = 64
MLA_V = 128
MLA_WIDTH = MLA_HEADS * MLA_V
ROPE_BASE = 10000.0
CONV_W = 3
N_BRANCH = 3

LANES = 128
SUBLANES = 8
VMEM_LIMIT = 56 * 1024 * 1024

Z_HQ, Z_HF, Z_HI, Z_HG = 0, 1024, 2048, 3072
Z_R, Z_K, Z_V = 4096, 5120, 6144
Z_CQ, Z_CKV = 7168, 7680
Z_GL = 8192
Z_WA = 8448
Z_KR = 8576
Z_GATES = 8704
Z_COLS = Z_GATES + N_BRANCH * 2048

RW_CHUNK = 64
HG_CHUNK = 64
HG_BASE = 8
HG_PASSES = 1


def _params(sem):
    return pltpu.CompilerParams(dimension_semantics=sem, vmem_limit_bytes=VMEM_LIMIT)


def _pick(n, cands):
    for c in cands:
        if n % c == 0:
            return c
    raise ValueError(f"no tile for {n} in {cands}")


def _block_id(idx, size):
    return lax.shift_right_logical(idx, int(np.log2(size)))


def _softplus(x):
    return jnp.maximum(x, 0.0) + jnp.log(1.0 + jnp.exp(-jnp.abs(x)))


def _row_valid(row, tp, pad, nb):
    valid = None
    for b in range(nb):
        ok = jnp.logical_or(row < b * tp, row >= b * tp + pad)
        valid = ok if valid is None else jnp.logical_and(valid, ok)
    return valid


def _normed_rows(x, g, eps, row0, tp, pad, nb):
    ms = jnp.mean(x * x, axis=-1, keepdims=True)
    y = x * lax.rsqrt(ms + eps) * g
    if pad:
        row = row0 + lax.broadcasted_iota(jnp.int32, (x.shape[0], 1), 0)
        y = jnp.where(_row_valid(row, tp, pad, nb), y, 0.0)
    return y


def _norm_matmul_kernel(x_ref, g_ref, w_ref, o_ref, xn_ref, *, tm, tp, pad, nb, eps):
    i = pl.program_id(0)

    @pl.when(pl.program_id(1) == 0)
    def _():
        xn_ref[...] = _normed_rows(x_ref[...], g_ref[...], eps, i * tm, tp, pad, nb).astype(BF16)

    o_ref[...] = jnp.dot(xn_ref[...], w_ref[...], preferred_element_type=F32).astype(o_ref.dtype)


def _norm_matmul(x, col_block, k, g, w, out_dtype, *, tp, pad, nb, tm, tn):
    m = x.shape[0]
    n = w.shape[1]
    kern = functools.partial(_norm_matmul_kernel, tm=tm, tp=tp, pad=pad, nb=nb, eps=EPS)
    return pl.pallas_call(
        kern,
        grid=(m // tm, n // tn),
        in_specs=[pl.BlockSpec((tm, k), lambda i, j: (i, col_block)),
                  pl.BlockSpec((1, k), lambda i, j: (0, 0)),
                  pl.BlockSpec((k, tn), lambda i, j: (0, j))],
        out_specs=pl.BlockSpec((tm, tn), lambda i, j: (i, j)),
        out_shape=jax.ShapeDtypeStruct((m, n), out_dtype),
        scratch_shapes=[pltpu.VMEM((tm, k), BF16)],
        compiler_params=_params(("arbitrary", "arbitrary")),
        name="norm_matmul",
    )(x, g.reshape(1, k).astype(F32), w)


def _hgrn_kernel(q_ref, f_ref, i_ref, g_ref, lb_ref, ng_ref, o_ref, st_ref, *, chunk, pad, nheads):
    c = pl.program_id(1)

    @pl.when(c == 0)
    def _():
        st_ref[...] = jnp.zeros_like(st_ref)

    L = chunk
    base = HG_BASE
    heads = range(nheads)
    hs = [slice(h * HG_DIM, (h + 1) * HG_DIM) for h in heads]
    q = q_ref[...]
    zf = f_ref[...]
    v = i_ref[...]
    lb = lb_ref[...]
    row = lax.broadcasted_iota(jnp.int32, (L, 1), 0)
    valid = c * L + row >= pad
    f = lb + (1.0 - lb) * jax.nn.sigmoid(zf)
    logf = jnp.where(valid, jnp.log(jnp.maximum(f, F_FLOOR)), 0.0)
    k = jnp.where(valid, (1.0 - lb) * jax.nn.sigmoid(-zf), 0.0)
    tri = (row >= lax.broadcasted_iota(jnp.int32, (L, L), 1)).astype(F32)
    b = _mm(tri, logf)

    r = row & (base - 1)
    o = [jnp.zeros((L, HG_DIM), F32) for _ in heads]
    for lag in range(base):
        ks, bs, vs = (k, b, v) if lag == 0 else tuple(pltpu.roll(x, lag, 0) for x in (k, b, v))
        ok = r >= lag
        term = q * ks * jnp.exp(jnp.where(ok, b - bs, 0.0))
        for h in heads:
            a = jnp.sum(term[:, hs[h]], axis=-1, keepdims=True)
            o[h] = o[h] + jnp.where(ok, a, 0.0) * vs[:, hs[h]]

    size = L
    while size > base:
        half = size // 2
        blocks = []
        for lo in range(0, L, size):
            mid, hi = lo + half, lo + size
            b_mid = b[mid - 1:mid, :]
            blocks.append((lo, mid, q[mid:hi] * jnp.exp(b[mid:hi] - b_mid), k[lo:mid] * jnp.exp(b_mid - b[lo:mid])))
        sc = [[_mm_nt(qg[:, hs[h]], kg[:, hs[h]], HG_PASSES) for h in heads] for (_, _, qg, kg) in blocks]
        pv = [[_mm(sc[bi][h], v[lo:mid, hs[h]], HG_PASSES) for h in heads] for bi, (lo, mid, _, _) in enumerate(blocks)]
        zero = jnp.zeros((half, HG_DIM), F32)
        for h in heads:
            o[h] = o[h] + jnp.concatenate([x for bi in range(len(blocks)) for x in (zero, pv[bi][h])], axis=0)
        size = half

    b_last = b[L - 1:L, :]
    q_dec = q * jnp.exp(b)
    k_end = k * jnp.exp(b_last - b)
    w_end = jnp.exp(b_last)
    st = [st_ref[h] for h in heads]
    o_st = [_mm_nt(q_dec[:, hs[h]], st[h], HG_PASSES) for h in heads]
    upd = [_mm(v[:, hs[h]].T, k_end[:, hs[h]], HG_PASSES) for h in heads]
    for h in heads:
        st_ref[h] = st[h] * w_end[:, hs[h]] + upd[h]

    zg = g_ref[...]
    gate = zg * jax.nn.sigmoid(zg)
    ng = ng_ref[...]
    for h in heads:
        y = o[h] + o_st[h]
        y = y * lax.rsqrt(jnp.mean(y * y, axis=-1, keepdims=True) + EPS) * ng[:, hs[h]]
        o_ref[:, hs[h]] = (y * gate[:, hs[h]]).astype(o_ref.dtype)


def _hgrn2(z, lb, norm_g, *, nb, tp, pad):
    m = z.shape[0]
    nc = tp // HG_CHUNK
    kern = functools.partial(_hgrn_kernel, chunk=HG_CHUNK, pad=pad, nheads=HG_HEADS)

    def zspec(off):
        return pl.BlockSpec((HG_CHUNK, HG_WIDTH), lambda b, c: (b * nc + c, off // HG_WIDTH))

    pspec = pl.BlockSpec((1, HG_WIDTH), lambda b, c: (0, 0))
    return pl.pallas_call(
        kern,
        grid=(nb, nc),
        in_specs=[zspec(Z_HQ), zspec(Z_HF), zspec(Z_HI), zspec(Z_HG), pspec, pspec],
        out_specs=pl.BlockSpec((HG_CHUNK, HG_WIDTH), lambda b, c: (b * nc + c, 0)),
        out_shape=jax.ShapeDtypeStruct((m, HG_WIDTH), BF16),
        scratch_shapes=[pltpu.VMEM((HG_HEADS, HG_DIM, HG_DIM), F32)],
        compiler_params=_params(("arbitrary", "arbitrary")),
        name="hgrn2",
    )(z, z, z, z, lb.reshape(1, HG_WIDTH).astype(F32), norm_g.reshape(1, HG_WIDTH).astype(F32))


def _rwkv_pre_kernel(zr_ref, zk_ref, zv_ref, zg_ref, zwa_ref,
                     mr_ref, mk_ref, mv_ref, mg_ref, mwa_ref,
                     w0_ref, w2_ref, a0_ref, a2_ref, g2_ref, kk_ref, ka_ref,
                     r_out, lw_out, k_out, v_out, kk_out, a_out, g_out,
                     er_ref, ek_ref, ev_ref, eg_ref, ewa_ref, *, tm, tp, pad, nb):
    i = pl.program_id(0)

    def shifted(z_ref, e_ref, mu_ref):
        @pl.when(i == 0)
        def _():
            e_ref[0:SUBLANES, :] = jnp.zeros((SUBLANES, e_ref.shape[1]), F32)

        x = z_ref[...]
        e_ref[SUBLANES:SUBLANES + tm, :] = x
        prev = e_ref[SUBLANES - 1:SUBLANES - 1 + tm, :]
        e_ref[0:SUBLANES, :] = x[tm - SUBLANES:tm, :]
        return x + mu_ref[...] * (prev - x)

    xr = shifted(zr_ref, er_ref, mr_ref)
    xk = shifted(zk_ref, ek_ref, mk_ref)
    xv = shifted(zv_ref, ev_ref, mv_ref)
    xg = shifted(zg_ref, eg_ref, mg_ref)
    xwa = shifted(zwa_ref, ewa_ref, mwa_ref)

    row = i * tm + lax.broadcasted_iota(jnp.int32, (tm, 1), 0)
    valid = _row_valid(row, tp, pad, nb)

    w = w0_ref[...] + jnp.dot(jnp.tanh(xwa), w2_ref[...], precision=HIGHEST, preferred_element_type=F32)
    w = -_softplus(-w) - 0.5
    lw_out[...] = -jnp.exp(w)
    a = jax.nn.sigmoid(a0_ref[...] + jnp.dot(xwa, a2_ref[...], precision=HIGHEST, preferred_element_type=F32))
    a_out[...] = a.astype(a_out.dtype)
    g_out[...] = jnp.dot(jax.nn.sigmoid(xg).astype(BF16), g2_ref[...],
                         preferred_element_type=F32).astype(g_out.dtype)
    r_out[...] = xr.astype(r_out.dtype)
    v_out[...] = jnp.where(valid, xv, 0.0).astype(v_out.dtype)
    k_out[...] = jnp.where(valid, xk * (1.0 + (a - 1.0) * ka_ref[...]), 0.0).astype(k_out.dtype)
    kk_out[...] = jnp.where(valid, xk * kk_ref[...], 0.0).astype(kk_out.dtype)


def _rwkv_pre(z, mu, w0, w2, a0, a2, g2, k_k, k_a, *, nb, tp, pad, tm):
    m = z.shape[0]
    mu_r, mu_k, mu_v = (mu[s:s + RW_WIDTH].reshape(1, RW_WIDTH) for s in (0, RW_WIDTH, 2 * RW_WIDTH))
    o = 3 * RW_WIDTH
    mu_wa = mu[o:o + 128].reshape(1, 128)
    mu_g = jnp.pad(mu[o + 128:o + 128 + RW_GATE_LORA], (0, 256 - RW_GATE_LORA)).reshape(1, 256)
    w2p = jnp.concatenate([w2, jnp.zeros_like(a2)], axis=0)
    a2p = jnp.concatenate([jnp.zeros_like(w2), a2], axis=0)
    g2p = jnp.pad(g2, ((0, 256 - RW_GATE_LORA), (0, 0)))
    kern = functools.partial(_rwkv_pre_kernel, tm=tm, tp=tp, pad=pad, nb=nb)

    def zspec(off, width):
        return pl.BlockSpec((tm, width), lambda i: (i, off // width))

    def full(shape):
        return pl.BlockSpec(shape, lambda i: (0,) * len(shape))

    vec = full((1, RW_WIDTH))
    wide = pl.BlockSpec((tm, RW_WIDTH), lambda i: (i, 0))
    outs = pl.pallas_call(
        kern,
        grid=(m // tm,),
        in_specs=[zspec(Z_R, RW_WIDTH), zspec(Z_K, RW_WIDTH), zspec(Z_V, RW_WIDTH), zspec(Z_GL, 256), zspec(Z_WA, 128),
                  vec, vec, vec, full((1, 256)), full((1, 128)),
                  vec, full((128, RW_WIDTH)), vec, full((128, RW_WIDTH)), full((256, RW_WIDTH)), vec, vec],
        out_specs=[wide] * 7,
        out_shape=[jax.ShapeDtypeStruct((m, RW_WIDTH), F32 if n == 1 else BF16) for n in range(7)],
        scratch_shapes=[pltpu.VMEM((tm + SUBLANES, RW_WIDTH), F32)] * 3
        + [pltpu.VMEM((tm + SUBLANES, 256), F32), pltpu.VMEM((tm + SUBLANES, 128), F32)],
        compiler_params=_params(("arbitrary",)),
        name="rwkv_pre",
    )(z, z, z, z, z, mu_r, mu_k, mu_v, mu_g, mu_wa,
      w0.reshape(1, -1), w2p, a0.reshape(1, -1), a2p, g2p.astype(BF16), k_k.reshape(1, -1), k_a.reshape(1, -1))
    return outs


_NN = (((1,), (0,)), ((), ()))
_NT = (((1,), (1,)), ((), ()))


def _split2(a):
    hi = a.astype(BF16)
    return hi, (a - hi.astype(F32)).astype(BF16)


def _dot(a, b, dims, passes):
    if passes == 6:
        return lax.dot_general(a, b, dims, precision=HIGHEST, preferred_element_type=F32)
    if passes == 1:
        return lax.dot_general(a.astype(BF16), b.astype(BF16), dims, preferred_element_type=F32)
    ah, al = _split2(a)
    bh, bl = _split2(b)
    out = lax.dot_general(ah, bh, dims, preferred_element_type=F32)
    out = out + lax.dot_general(ah, bl, dims, preferred_element_type=F32)
    return out + lax.dot_general(al, bh, dims, preferred_element_type=F32)


def _mm(a, b, passes=6):
    return _dot(a, b, _NN, passes)


def _mm_nt(a, b, passes=6):
    return _dot(a, b, _NT, passes)


def _bf(x):
    return x.astype(BF16)


def _bmm(a, b):
    return lax.dot_general(a, b, _NN, preferred_element_type=F32)


def _bmm_nt(a, b):
    return lax.dot_general(a, b, _NT, preferred_element_type=F32)


def _geometric_sums(xs, eye, steps):
    n = xs[0].shape[0]
    s = [eye + x for x in xs]
    if steps == 1:
        return s
    p = [_bmm(b, b) for b in map(_bf, xs)]
    for k in range(1, steps):
        pb = [_bf(pp) for pp in p]
        if k == steps - 1:
            s = [ss + _bmm(b, _bf(ss)) for ss, b in zip(s, pb)]
        else:
            both = [_bmm(b, jnp.concatenate([b, _bf(ss)], axis=1)) for b, ss in zip(pb, s)]
            p = [r[:, 0:n] for r in both]
            s = [ss + r[:, n:2 * n] for ss, r in zip(s, both)]
    return s


def _unit_lower_inverse(n_mats, blk):
    n = n_mats[0].shape[0]
    ri = lax.broadcasted_iota(jnp.int32, (n, n), 0)
    ci = lax.broadcasted_iota(jnp.int32, (n, n), 1)
    eye = (ri == ci).astype(F32)
    diag_blk = _block_id(ri, blk) == _block_id(ci, blk)
    d_inv = _geometric_sums([jnp.where(diag_blk, -m, 0.0) for m in n_mats], eye, int(np.log2(blk)))
    d_inv_b = [_bf(d) for d in d_inv]
    m2 = [-_bmm(d, _bf(jnp.where(diag_blk, 0.0, m))) for d, m in zip(d_inv_b, n_mats)]
    t = _geometric_sums(m2, eye, int(np.log2(RW_CHUNK // blk)))
    return [_bmm(_bf(tt), d) for tt, d in zip(t, d_inv_b)]


def _rwkv_rec_kernel(r_ref, lw_ref, k_ref, v_ref, kk_ref, a_ref, g_ref, rk_ref, lng_ref, lnb_ref,
                     o_ref, st_ref, *, chunk, npair):
    @pl.when(pl.program_id(1) == 0)
    def _():
        st_ref[...] = jnp.zeros_like(st_ref)

    L = chunk
    n2 = 2 * L
    lane = lax.broadcasted_iota(jnp.int32, (1, LANES), 1)
    m0 = (lane < RW_HEAD).astype(F32)
    m1 = 1.0 - m0
    tri = (lax.broadcasted_iota(jnp.int32, (L, L), 0) >= lax.broadcasted_iota(jnp.int32, (L, L), 1)).astype(F32)
    ri = lax.broadcasted_iota(jnp.int32, (n2, n2), 0)
    ci = lax.broadcasted_iota(jnp.int32, (n2, n2), 1)
    strict = ri > ci
    incl = ri >= ci
    cum_all = _mm(tri, lw_ref[...])

    def stack(x):
        return jnp.concatenate([x * m0, x * m1], axis=0)

    def head_mean(x):
        s0 = jnp.sum(x * m0, axis=-1, keepdims=True)
        s1 = jnp.sum(x * m1, axis=-1, keepdims=True)
        return (s0 * m0 + s1 * m1) * (1.0 / RW_HEAD)

    pairs = range(npair)
    sls = [slice(p * LANES, (p + 1) * LANES) for p in pairs]
    ops = []
    for sl in sls:
        r = r_ref[:, sl].astype(F32)
        lw = lw_ref[:, sl]
        k = k_ref[:, sl].astype(F32)
        kk = kk_ref[:, sl].astype(F32)
        kn = kk / jnp.maximum(jnp.sqrt(head_mean(kk * kk) * RW_HEAD + 1e-24), 1e-12)
        cum = cum_all[:, sl]
        c_last = cum[L - 1:L, :]
        e_neg = jnp.exp(-cum)
        e_end = jnp.exp(c_last - cum)
        beta = kn * a_ref[:, sl].astype(F32)
        v_s = stack(v_ref[:, sl].astype(F32))
        ops.append(dict(
            al_t=_bf(stack(kn * jnp.exp(cum - lw))),
            r_t=_bf(stack(r * jnp.exp(cum))),
            be_h=_bf(stack(beta * e_neg)),
            k_h=_bf(stack(k * e_neg)),
            be_e=_bf(stack(beta * e_end)),
            k_e=_bf(stack(k * e_end)),
            v_s=v_s, v_b=_bf(v_s),
            w_end=jnp.exp(c_last)))

    sc = [_bmm_nt(jnp.concatenate([o["al_t"], o["r_t"]], axis=0), jnp.concatenate([o["be_h"], o["k_h"]], axis=0))
          for o in ops]
    a_ab = [jnp.where(strict, s[0:n2, 0:n2], 0.0) for s in sc]
    a_ak = [_bf(jnp.where(strict, s[0:n2, n2:2 * n2], 0.0)) for s in sc]
    a_rb = [_bf(jnp.where(incl, s[n2:2 * n2, 0:n2], 0.0)) for s in sc]
    a_rk = [_bf(jnp.where(incl, s[n2:2 * n2, n2:2 * n2], 0.0)) for s in sc]
    gy = [_bmm(jnp.concatenate([ak, rk], axis=0), o["v_b"]) for ak, rk, o in zip(a_ak, a_rk, ops)]
    t_inv = _unit_lower_inverse(a_ab, 16)
    hats = [_bmm(_bf(t), jnp.concatenate([o["al_t"], _bf(g[0:n2])], axis=1)) for t, o, g in zip(t_inv, ops, gy)]

    st = [st_ref[p] for p in pairs]
    from_st = [_bmm_nt(jnp.concatenate([_bf(h[:, 0:LANES]), o["r_t"]], axis=0), _bf(s))
               for h, o, s in zip(hats, ops, st)]
    u = [-(f[0:n2] + h[:, LANES:2 * LANES]) for f, h in zip(from_st, hats)]
    y_u = [_bmm(a, _bf(uu)) for a, uu in zip(a_rb, u)]
    upd = [_bmm(_bf(jnp.concatenate([uu, o["v_s"]], axis=0).T), jnp.concatenate([o["be_e"], o["k_e"]], axis=0))
           for uu, o in zip(u, ops)]
    for p in pairs:
        st_ref[p] = st[p] * ops[p]["w_end"] + upd[p]
    y_st = [f[n2:2 * n2] for f in from_st]
    y_loc = [g[n2:2 * n2] for g in gy]

    for p, sl in enumerate(sls):
        y = y_st[p] + y_u[p] + y_loc[p]
        y = y[0:L] + y[L:n2]
        r = r_ref[:, sl].astype(F32)
        k = k_ref[:, sl].astype(F32)
        v = v_ref[:, sl].astype(F32)
        mu = head_mean(y)
        yc = y - mu
        var = head_mean(yc * yc)
        yn = yc * lax.rsqrt(var + RW_GN_EPS) * lng_ref[:, sl] + lnb_ref[:, sl]
        bonus = head_mean(r * k * rk_ref[:, sl]) * RW_HEAD
        o_ref[:, sl] = ((yn + bonus * v) * g_ref[:, sl]).astype(o_ref.dtype)


def _rwkv_rec(pre, r_k, ln_g, ln_b, *, nb, tp):
    r, lw, k, v, kn, a, g = pre
    m = r.shape[0]
    L = RW_CHUNK
    nc = tp // L
    npair = RW_WIDTH // LANES
    kern = functools.partial(_rwkv_rec_kernel, chunk=L, npair=npair)
    blk = pl.BlockSpec((L, RW_WIDTH), lambda b, c: (b * nc + c, 0))
    pspec = pl.BlockSpec((1, RW_WIDTH), lambda b, c: (0, 0))
    return pl.pallas_call(
        kern,
        grid=(nb, nc),
        in_specs=[blk] * 7 + [pspec] * 3,
        out_specs=blk,
        out_shape=jax.ShapeDtypeStruct((m, RW_WIDTH), BF16),
        scratch_shapes=[pltpu.VMEM((npair, LANES, LANES), F32)],
        compiler_params=_params(("arbitrary", "arbitrary")),
        name="rwkv_rec",
    )(r, lw, k, v, kn, a, g, r_k.reshape(1, -1).astype(F32), ln_g.reshape(1, -1).astype(F32),
      ln_b.reshape(1, -1).astype(F32))


def _rope_kernel(x_ref, cos_ref, sin_ref, o_ref, *, n_plain, n_rope, scale, is_query, tr, nrb, pad):
    lane = lax.broadcasted_iota(jnp.int32, (1, LANES), 1)
    cos = cos_ref[...]
    sin = sin_ref[...]
    if is_query:
        bias = jnp.ones((tr, 1), F32)
    else:
        pos = (pl.program_id(0) % nrb) * tr + lax.broadcasted_iota(jnp.int32, (tr, 1), 0)
        bias = jnp.where(pos < pad, NEG_BIG, 0.0)
    if n_plain:
        o_ref[:, 0:n_plain * LANES] = (x_ref[:, 0:n_plain * LANES] * scale).astype(o_ref.dtype)
    for h in range(n_plain, n_plain + n_rope):
        x = x_ref[:, h * LANES:(h + 1) * LANES]
        rot = jnp.where(lane < MLA_ROPE // 2, -pltpu.roll(x, LANES - MLA_ROPE // 2, 1),
                        pltpu.roll(x, MLA_ROPE // 2, 1))
        y = (x * cos + rot * sin) * scale
        o_ref[:, h * LANES:(h + 1) * LANES] = jnp.where(lane == MLA_ROPE, bias, y).astype(o_ref.dtype)


def _rope(x, col_block, n_plain, n_rope, cos, sin, scale, is_query, *, tp, tr, pad):
    m = x.shape[0]
    nrb = tp // tr
    width = (n_plain + n_rope) * LANES
    kern = functools.partial(_rope_kernel, n_plain=n_plain, n_rope=n_rope, scale=scale, is_query=is_query,
                             tr=tr, nrb=nrb, pad=pad)
    tab = pl.BlockSpec((tr, LANES), lambda i: (i % nrb, 0))
    return pl.pallas_call(
        kern,
        grid=(m // tr,),
        in_specs=[pl.BlockSpec((tr, width), lambda i: (i, col_block)), tab, tab],
        out_specs=pl.BlockSpec((tr, width), lambda i: (i, 0)),
        out_shape=jax.ShapeDtypeStruct((m, width), BF16),
        compiler_params=_params(("arbitrary",)),
        name="rope",
    )(x, cos, sin)


def _attn_kernel(qn_ref, qr_ref, kn_ref, kr_ref, vt_ref, bias_ref, o_ref,
                 s0_ref, s1_ref, m_ref, l_ref, acc_ref, *, t, nq):
    qi = pl.program_id(2)
    q = jnp.concatenate([qn_ref[...], qr_ref[...]], axis=1)

    def scores_t(off):
        kcat = jnp.concatenate([kn_ref[pl.ds(off, 4 * t), :], kr_ref[pl.ds(off, 4 * t), :]], axis=1)
        return lax.dot_general(kcat, q, _NT, preferred_element_type=F32)

    def first_tile(jq):
        return jnp.minimum(4 * jq, nq - 4)

    def issue_scores(jq, dst_ref):
        first = first_tile(jq)
        s = scores_t(pl.multiple_of(first * t, t))
        for n in range(4):
            x = first + n
            kind = jnp.where(x < 4 * jq, 2, jnp.where(x < qi, 0, jnp.where(x == qi, 1, 2)))
            dst_ref[n * t:(n + 1) * t, :] = s[n * t:(n + 1) * t] + bias_ref[kind]

    def consume(jq, src_ref):
        first = first_tile(jq)
        for stream in range(2):
            s = src_ref[2 * stream * t:2 * (stream + 1) * t, :]
            m_i = m_ref[stream]
            m_new = jnp.maximum(m_i, jnp.max(s, axis=0, keepdims=True))
            alpha = jnp.exp2(m_i - m_new)
            p = jnp.exp2(s - m_new)
            m_ref[stream] = m_new
            l_ref[stream] = alpha * l_ref[stream] + jnp.sum(p, axis=0, keepdims=True)
            v_t = jnp.concatenate([vt_ref[0, first + 2 * stream], vt_ref[0, first + 2 * stream + 1]], axis=1)
            acc_ref[stream] = alpha * acc_ref[stream] + jnp.dot(v_t, p.astype(BF16), preferred_element_type=F32)

    def half(jq, cur_ref, nxt_ref):
        issue_scores(jq + 1, nxt_ref)
        consume(jq, cur_ref)

    n_groups = qi // 4 + 1
    m_ref[...] = jnp.full(m_ref.shape, NEG_BIG, F32)
    l_ref[...] = jnp.zeros(l_ref.shape, F32)
    acc_ref[...] = jnp.zeros(acc_ref.shape, F32)
    issue_scores(0, s0_ref)

    def trip(r, carry):
        half(2 * r, s0_ref, s1_ref)

        @pl.when(2 * r + 1 < n_groups)
        def _():
            half(2 * r + 1, s1_ref, s0_ref)

        return carry

    lax.fori_loop(0, (n_groups + 1) // 2, trip, 0)
    m_a, m_b = m_ref[0], m_ref[1]
    m_i = jnp.maximum(m_a, m_b)
    w_a = jnp.exp2(m_a - m_i)
    w_b = jnp.exp2(m_b - m_i)
    out_t = (w_a * acc_ref[0] + w_b * acc_ref[1]) / (w_a * l_ref[0] + w_b * l_ref[1])
    o_ref[...] = out_t.T.astype(o_ref.dtype)


def _attention(q, kr, kv, *, nb, tp):
    m = q.shape[0]
    tq = _pick(tp, tuple(c for c in (384, 256, 128) if tp >= 4 * c))
    nq = tp // tq
    kern = functools.partial(_attn_kernel, t=tq, nq=nq)
    allowed_t = jnp.arange(tq)[:, None] <= jnp.arange(tq)[None, :]
    bias = jnp.stack([jnp.zeros((tq, tq), F32), jnp.where(allowed_t, 0.0, NEG_BIG).astype(F32),
                      jnp.full((tq, tq), NEG_BIG, F32)])
    v_t = kv[:, MLA_HEADS * MLA_NOPE:].reshape(nb, nq, tq, MLA_HEADS, MLA_V)
    v_t = v_t.transpose(0, 3, 1, 4, 2).reshape(nb * MLA_HEADS, nq, MLA_V, tq)
    return pl.pallas_call(
        kern,
        grid=(nb, MLA_HEADS, nq),
        in_specs=[pl.BlockSpec((tq, LANES), lambda b, h, i: (b * nq + i, h)),
                  pl.BlockSpec((tq, LANES), lambda b, h, i: (b * nq + i, MLA_HEADS + h)),
                  pl.BlockSpec((tp, LANES), lambda b, h, i: (b, h)),
                  pl.BlockSpec((tp, LANES), lambda b, h, i: (b, 0)),
                  pl.BlockSpec((1, nq, MLA_V, tq), lambda b, h, i: (b * MLA_HEADS + h, 0, 0, 0)),
                  pl.BlockSpec((3, tq, tq), lambda b, h, i: (0, 0, 0))],
        out_specs=pl.BlockSpec((tq, LANES), lambda b, h, i: (b * nq + i, h)),
        out_shape=jax.ShapeDtypeStruct((m, MLA_WIDTH), BF16),
        scratch_shapes=[pltpu.VMEM((4 * tq, tq), F32), pltpu.VMEM((4 * tq, tq), F32),
                        pltpu.VMEM((2, 1, tq), F32), pltpu.VMEM((2, 1, tq), F32), pltpu.VMEM((2, MLA_V, tq), F32)],
        compiler_params=_params(("arbitrary", "arbitrary", "arbitrary")),
        name="mla_attention",
    )(q, q, kv, kr, v_t, bias)


def _merge_kernel(ya_ref, yb_ref, yc_ref, ga_ref, gb_ref, gc_ref, w_ref, o_ref):
    acc = None
    for n, (y_ref, g_ref) in enumerate(((ya_ref, ga_ref), (yb_ref, gb_ref), (yc_ref, gc_ref))):
        p = jnp.dot(y_ref[...], w_ref[n], preferred_element_type=F32)
        t = jax.nn.sigmoid(g_ref[...]) * p
        acc = t if acc is None else acc + t
    o_ref[...] = acc.astype(o_ref.dtype)


def _merge(ya, yb, yc, z, w_branch, *, tm, tn):
    m = ya.shape[0]
    d = w_branch.shape[2]
    yspec = pl.BlockSpec((tm, HG_WIDTH), lambda i, j: (i, 0))

    def gspec(n):
        return pl.BlockSpec((tm, tn), lambda i, j: (i, (Z_GATES + n * d) // tn + j))

    return pl.pallas_call(
        _merge_kernel,
        grid=(m // tm, d // tn),
        in_specs=[yspec, yspec, yspec, gspec(0), gspec(1), gspec(2),
                  pl.BlockSpec((N_BRANCH, HG_WIDTH, tn), lambda i, j: (0, 0, j))],
        out_specs=pl.BlockSpec((tm, tn), lambda i, j: (i, j)),
        out_shape=jax.ShapeDtypeStruct((m, d), BF16),
        compiler_params=_params(("arbitrary", "arbitrary")),
        name="merge",
    )(ya, yb, yc, z, z, z, w_branch)


def _matmul_res_kernel(x_ref, w_ref, r_ref, o_ref):
    o_ref[...] = r_ref[...] + jnp.dot(x_ref[...], w_ref[...], preferred_element_type=F32)


def _matmul_res(x, w, res, *, tm, tn):
    m, k = x.shape
    n = w.shape[1]
    return pl.pallas_call(
        _matmul_res_kernel,
        grid=(m // tm, n // tn),
        in_specs=[pl.BlockSpec((tm, k), lambda i, j: (i, 0)),
                  pl.BlockSpec((k, tn), lambda i, j: (0, j)),
                  pl.BlockSpec((tm, tn), lambda i, j: (i, j))],
        out_specs=pl.BlockSpec((tm, tn), lambda i, j: (i, j)),
        out_shape=jax.ShapeDtypeStruct((m, n), F32),
        compiler_params=_params(("arbitrary", "arbitrary")),
        name="matmul_res",
    )(x, w, res)


def _ffn_up_kernel(x_ref, g_ref, wg_ref, wv_ref, cwg_ref, cwv_ref, cbg_ref, cbv_ref, o_ref,
                   xn_ref, eg_ref, ev_ref, pg_ref, pv_ref, *, tm, tp, pad, nb, eps):
    i = pl.program_id(0)
    j = pl.program_id(1)

    @pl.when(j == 0)
    def _():
        xn_ref[...] = _normed_rows(x_ref[...], g_ref[...], eps, i * tm, tp, pad, nb).astype(BF16)

    @pl.when(i == 0)
    def _():
        pg_ref[j] = jnp.zeros(pg_ref.shape[1:], F32)
        pv_ref[j] = jnp.zeros(pv_ref.shape[1:], F32)

    def conv(w_ref, e_ref, p_ref, cw_ref, cb_ref):
        u = jnp.dot(xn_ref[...], w_ref[...], preferred_element_type=F32)
        e_ref[0:SUBLANES, :] = p_ref[j]
        e_ref[SUBLANES:SUBLANES + tm, :] = u
        p_ref[j] = u[tm - SUBLANES:tm, :]
        u1 = e_ref[SUBLANES - 1:SUBLANES - 1 + tm, :]
        u2 = e_ref[SUBLANES - 2:SUBLANES - 2 + tm, :]
        cw = cw_ref[...]
        return cw[0:1, :] * u2 + cw[1:2, :] * u1 + cw[2:3, :] * u + cb_ref[...]

    gate = conv(wg_ref, eg_ref, pg_ref, cwg_ref, cbg_ref)
    val = conv(wv_ref, ev_ref, pv_ref, cwv_ref, cbv_ref)
    o_ref[...] = (gate * jax.nn.sigmoid(gate) * val).astype(o_ref.dtype)


def _ffn_up(h, g, w_up, conv_w, conv_b, *, tp, pad, nb, tm, tf):
    m, d = h.shape
    dff = w_up.shape[1] // 2
    nj = dff // tf
    kern = functools.partial(_ffn_up_kernel, tm=tm, tp=tp, pad=pad, nb=nb, eps=EPS)
    cb = conv_b.reshape(1, -1).astype(F32)
    return pl.pallas_call(
        kern,
        grid=(m // tm, nj),
        in_specs=[pl.BlockSpec((tm, d), lambda i, j: (i, 0)),
                  pl.BlockSpec((1, d), lambda i, j: (0, 0)),
                  pl.BlockSpec((d, tf), lambda i, j: (0, j)),
                  pl.BlockSpec((d, tf), lambda i, j: (0, nj + j)),
                  pl.BlockSpec((CONV_W, tf), lambda i, j: (0, j)),
                  pl.BlockSpec((CONV_W, tf), lambda i, j: (0, nj + j)),
                  pl.BlockSpec((1, tf), lambda i, j: (0, j)),
                  pl.BlockSpec((1, tf), lambda i, j: (0, nj + j))],
        out_specs=pl.BlockSpec((tm, tf), lambda i, j: (i, j)),
        out_shape=jax.ShapeDtypeStruct((m, dff), BF16),
        scratch_shapes=[pltpu.VMEM((tm, d), BF16),
                        pltpu.VMEM((tm + SUBLANES, tf), F32), pltpu.VMEM((tm + SUBLANES, tf), F32),
                        pltpu.VMEM((nj, SUBLANES, tf), F32), pltpu.VMEM((nj, SUBLANES, tf), F32)],
        compiler_params=_params(("arbitrary", "arbitrary")),
        name="ffn_up",
    )(h, g.reshape(1, d).astype(F32), w_up, w_up, conv_w.astype(F32), conv_w.astype(F32), cb, cb)


def _final_norm_kernel(x_ref, g_ref, o_ref):
    x = x_ref[0]
    o_ref[0] = x * lax.rsqrt(jnp.mean(x * x, axis=-1, keepdims=True) + EPS) * g_ref[...]


def _final_norm(h3, g, *, skip, seq):
    nb, tp, d = h3.shape
    tr = LANES
    return pl.pallas_call(
        _final_norm_kernel,
        grid=(nb, seq // tr),
        in_specs=[pl.BlockSpec((1, tr, d), lambda b, i: (b, skip // tr + i, 0)),
                  pl.BlockSpec((1, d), lambda b, i: (0, 0))],
        out_specs=pl.BlockSpec((1, tr, d), lambda b, i: (b, i, 0)),
        out_shape=jax.ShapeDtypeStruct((nb, seq, d), F32),
        compiler_params=_params(("arbitrary", "arbitrary")),
        name="final_norm",
    )(h3, g.reshape(1, d).astype(F32))


def _layout_w_in(w):
    d = w.shape[0]
    o = 0
    parts = {}
    for name, width in (("hg4", 4 * HG_WIDTH), ("rkv", 3 * RW_WIDTH), ("wl", RW_DECAY_LORA), ("al", RW_AAA_LORA),
                        ("gl", RW_GATE_LORA), ("cq", MLA_Q_RANK), ("ckv", MLA_KV_RANK), ("kr", MLA_ROPE),
                        ("gates", N_BRANCH * d)):
        parts[name] = w[:, o:o + width]
        o += width
    zeros = lambda n: jnp.zeros((d, n), w.dtype)
    cols = [parts["hg4"], parts["rkv"], parts["cq"], parts["ckv"],
            parts["gl"], zeros(256 - RW_GATE_LORA), parts["wl"], parts["al"],
            parts["kr"], zeros(LANES - MLA_ROPE), parts["gates"]]
    return jnp.concatenate(cols, axis=1).astype(BF16)


def _layout_w_uq(w):
    r = w.shape[0]
    w = w.reshape(r, MLA_HEADS, MLA_NOPE + MLA_ROPE)
    nope = w[:, :, :MLA_NOPE].reshape(r, MLA_HEADS * MLA_NOPE)
    rope = jnp.pad(w[:, :, MLA_NOPE:], ((0, 0), (0, 0), (0, LANES - MLA_ROPE))).reshape(r, MLA_HEADS * LANES)
    return jnp.concatenate([nope, rope], axis=1).astype(BF16)


def _layout_w_ukv(w):
    r = w.shape[0]
    w = w.reshape(r, MLA_HEADS, MLA_NOPE + MLA_V)
    return jnp.concatenate([w[:, :, :MLA_NOPE].reshape(r, -1), w[:, :, MLA_NOPE:].reshape(r, -1)], axis=1).astype(BF16)


def _rope_tables(tp, pad):
    inv = ROPE_BASE ** (-jnp.arange(0, MLA_ROPE, 2, dtype=F32) / MLA_ROPE)
    pos = jnp.arange(tp, dtype=F32) - float(pad)
    ang = pos[:, None] * inv[None, :]
    z = jnp.zeros((tp, LANES - MLA_ROPE), F32)
    cos = jnp.concatenate([jnp.cos(ang), jnp.cos(ang), z], axis=1)
    sin = jnp.concatenate([jnp.sin(ang), jnp.sin(ang), z], axis=1)
    return cos, sin


def kernel(x, meta_tokens, norm1_g, w_in, hg_lb_logits, hg_norm_g, rw_mu, rw_w0, rw_w2, rw_a0, rw_a2, rw_g2, rw_k_k, rw_k_a, rw_r_k, rw_ln_g, rw_ln_b, mla_q_norm_g, mla_w_uq, mla_kv_norm_g, mla_w_ukv, w_branch, w_out, norm2_g, ffn_w_up, ffn_conv_w, ffn_conv_b, ffn_w_down, final_norm_g):
    nb, seq, d = x.shape
    depth = w_in.shape[0]
    t = N_META + seq
    pad = (-N_META) % LANES
    tp = pad + t
    m = nb * tp
    tm = _pick(m, (768, 384, 256, 128))
    tm_wide = _pick(m, (1376, 768, 384, 256, 128))
    geo = dict(tp=tp, pad=pad, nb=nb)

    meta = jnp.broadcast_to(meta_tokens[None].astype(x.dtype), (nb, N_META, d))
    h = jnp.concatenate([jnp.zeros((nb, pad, d), x.dtype), meta, x], axis=1).reshape(m, d)
    cos, sin = _rope_tables(tp, pad)
    p_lb = jax.nn.softmax(hg_lb_logits.astype(F32), axis=0)
    lower_bounds = jnp.cumsum(p_lb, axis=0) - p_lb[0]
    scale = float((MLA_NOPE + MLA_ROPE) ** -0.5 * np.log2(np.e))

    for l in range(depth):
        z = _norm_matmul(h, 0, d, norm1_g[l], _layout_w_in(w_in[l]), F32, tm=tm_wide, tn=512, **geo)
        y_a = _hgrn2(z, lower_bounds[l], hg_norm_g[l], **geo)
        pre = _rwkv_pre(z, rw_mu[l], rw_w0[l], rw_w2[l], rw_a0[l], rw_a2[l], rw_g2[l], rw_k_k[l], rw_k_a[l],
                        tm=_pick(m, (384, 256, 128)), **geo)
        y_b = _rwkv_rec(pre, rw_r_k[l], rw_ln_g[l], rw_ln_b[l], nb=nb, tp=tp)
        q = _norm_matmul(z, Z_CQ // MLA_Q_RANK, MLA_Q_RANK, mla_q_norm_g[l], _layout_w_uq(mla_w_uq[l]), F32,
                         tm=tm, tn=512, tp=tp, pad=0, nb=nb)
        kv = _norm_matmul(z, Z_CKV // MLA_KV_RANK, MLA_KV_RANK, mla_kv_norm_g[l], _layout_w_ukv(mla_w_ukv[l]), BF16,
                          tm=tm, tn=512, tp=tp, pad=0, nb=nb)
        tr = _pick(tp, (384, 256, 128))
        q_cat = _rope(q, 0, MLA_HEADS, MLA_HEADS, cos, sin, scale, True, tp=tp, tr=tr, pad=pad)
        k_rope = _rope(z, Z_KR // LANES, 0, 1, cos, sin, 1.0, False, tp=tp, tr=tr, pad=pad)
        y_c = _attention(q_cat, k_rope, kv, nb=nb, tp=tp)
        merged = _merge(y_a, y_b, y_c, z, w_branch[l].astype(BF16), tm=tm, tn=512)
        h = _matmul_res(merged, w_out[l].astype(BF16), h, tm=tm, tn=512)
        act = _ffn_up(h, norm2_g[l], ffn_w_up[l].astype(BF16), ffn_conv_w[l], ffn_conv_b[l], tm=tm, tf=512, **geo)
        h = _matmul_res(act, ffn_w_down[l].astype(BF16), h, tm=tm, tn=512)
    return _final_norm(h.reshape(nb, tp, d), final_norm_g, skip=pad + N_META, seq=seq)
```

```python
import functools

import numpy as np
import jax
import jax.numpy as jnp
from jax import lax
from jax.experimental import pallas as pl
from jax.experimental.pallas import tpu as pltpu

F32 = jnp.float32
BF16 = jnp.bfloat16
HIGHEST = lax.Precision.HIGHEST

N_META = 16
EPS = 1e-6
NEG_BIG = -1e30
F_FLOOR = 1e-30
HG_HEADS = 8
HG_DIM = 128
HG_WIDTH = HG_HEADS * HG_DIM
RW_HEAD = 64
RW_HEADS = 16
RW_WIDTH = RW_HEADS * RW_HEAD
RW_DECAY_LORA = 64
RW_AAA_LORA = 64
RW_GATE_LORA = 160
RW_GN_EPS = 64e-5
MLA_HEADS = 8
MLA_Q_RANK = 512
MLA_KV_RANK = 512
MLA_NOPE = 128
MLA_ROPE = 64
MLA_V = 128
MLA_WIDTH = MLA_HEADS * MLA_V
ROPE_BASE = 10000.0
CONV_W = 3
N_BRANCH = 3

LANES = 128
SUBLANES = 8
VMEM_LIMIT = 56 * 1024 * 1024

Z_HQ, Z_HF, Z_HI, Z_HG = 0, 1024, 2048, 3072
Z_R, Z_K, Z_V = 4096, 5120, 6144
Z_CQ, Z_CKV = 7168, 7680
Z_GL = 8192
Z_WA = 8448
Z_KR = 8576
Z_GATES = 8704
Z_COLS = Z_GATES + N_BRANCH * 2048

RW_CHUNK = 64
HG_CHUNK = 64
HG_BASE = 8
HG_PASSES = 1


def _params(sem):
    return pltpu.CompilerParams(dimension_semantics=sem, vmem_limit_bytes=VMEM_LIMIT)


def _pick(n, cands):
    for c in cands:
        if n % c == 0:
            return c
    raise ValueError(f"no tile for {n} in {cands}")


def _block_id(idx, size):
    return lax.shift_right_logical(idx, int(np.log2(size)))


def _softplus(x):
    return jnp.maximum(x, 0.0) + jnp.log(1.0 + jnp.exp(-jnp.abs(x)))


def _row_valid(row, tp, pad, nb):
    valid = None
    for b in range(nb):
        ok = jnp.logical_or(row < b * tp, row >= b * tp + pad)
        valid = ok if valid is None else jnp.logical_and(valid, ok)
    return valid


def _normed_rows(x, g, eps, row0, tp, pad, nb):
    ms = jnp.mean(x * x, axis=-1, keepdims=True)
    y = x * lax.rsqrt(ms + eps) * g
    if pad:
        row = row0 + lax.broadcasted_iota(jnp.int32, (x.shape[0], 1), 0)
        y = jnp.where(_row_valid(row, tp, pad, nb), y, 0.0)
    return y


def _norm_matmul_kernel(x_ref, g_ref, w_ref, o_ref, xn_ref, *, tm, tp, pad, nb, eps):
    i = pl.program_id(0)

    @pl.when(pl.program_id(1) == 0)
    def _():
        xn_ref[...] = _normed_rows(x_ref[...], g_ref[...], eps, i * tm, tp, pad, nb).astype(BF16)

    o_ref[...] = jnp.dot(xn_ref[...], w_ref[...], preferred_element_type=F32).astype(o_ref.dtype)


def _norm_matmul(x, col_block, k, g, w, out_dtype, *, tp, pad, nb, tm, tn):
    m = x.shape[0]
    n = w.shape[1]
    kern = functools.partial(_norm_matmul_kernel, tm=tm, tp=tp, pad=pad, nb=nb, eps=EPS)
    return pl.pallas_call(
        kern,
        grid=(m // tm, n // tn),
        in_specs=[pl.BlockSpec((tm, k), lambda i, j: (i, col_block)),
                  pl.BlockSpec((1, k), lambda i, j: (0, 0)),
                  pl.BlockSpec((k, tn), lambda i, j: (0, j))],
        out_specs=pl.BlockSpec((tm, tn), lambda i, j: (i, j)),
        out_shape=jax.ShapeDtypeStruct((m, n), out_dtype),
        scratch_shapes=[pltpu.VMEM((tm, k), BF16)],
        compiler_params=_params(("arbitrary", "arbitrary")),
        name="norm_matmul",
    )(x, g.reshape(1, k).astype(F32), w)


def _hgrn_stages(q_ref, f_ref, i_ref, g_ref, lb_ref, ng_ref, o_ref, st_ref, *, chunk, pad, nheads):
    c = pl.program_id(1)

    @pl.when(c == 0)
    def _():
        st_ref[...] = jnp.zeros_like(st_ref)

    L = chunk
    base = HG_BASE
    heads = range(nheads)
    hs = [slice(h * HG_DIM, (h + 1) * HG_DIM) for h in heads]
    q = q_ref[...]
    zf = f_ref[...]
    v = i_ref[...]
    lb = lb_ref[...]
    row = lax.broadcasted_iota(jnp.int32, (L, 1), 0)
    valid = c * L + row >= pad
    f = lb + (1.0 - lb) * jax.nn.sigmoid(zf)
    logf = jnp.where(valid, jnp.log(jnp.maximum(f, F_FLOOR)), 0.0)
    k = jnp.where(valid, (1.0 - lb) * jax.nn.sigmoid(-zf), 0.0)
    tri = (row >= lax.broadcasted_iota(jnp.int32, (L, L), 1)).astype(F32)
    b = _mm(tri, logf)

    r = row & (base - 1)
    o = [jnp.zeros((L, HG_DIM), F32) for _ in heads]
    for lag in range(base):
        ks, bs, vs = (k, b, v) if lag == 0 else tuple(pltpu.roll(x, lag, 0) for x in (k, b, v))
        ok = r >= lag
        term = q * ks * jnp.exp(jnp.where(ok, b - bs, 0.0))
        for h in heads:
            a = jnp.sum(term[:, hs[h]], axis=-1, keepdims=True)
            o[h] = o[h] + jnp.where(ok, a, 0.0) * vs[:, hs[h]]
        yield

    size = L
    while size > base:
        half = size // 2
        blocks = []
        for lo in range(0, L, size):
            mid, hi = lo + half, lo + size
            b_mid = b[mid - 1:mid, :]
            blocks.append((lo, mid, q[mid:hi] * jnp.exp(b[mid:hi] - b_mid), k[lo:mid] * jnp.exp(b_mid - b[lo:mid])))
        sc = [[_mm_nt(qg[:, hs[h]], kg[:, hs[h]], HG_PASSES) for h in heads] for (_, _, qg, kg) in blocks]
        pv = [[_mm(sc[bi][h], v[lo:mid, hs[h]], HG_PASSES) for h in heads] for bi, (lo, mid, _, _) in enumerate(blocks)]
        zero = jnp.zeros((half, HG_DIM), F32)
        for h in heads:
            o[h] = o[h] + jnp.concatenate([x for bi in range(len(blocks)) for x in (zero, pv[bi][h])], axis=0)
        size = half
        yield

    b_last = b[L - 1:L, :]
    q_dec = q * jnp.exp(b)
    k_end = k * jnp.exp(b_last - b)
    w_end = jnp.exp(b_last)
    st = [st_ref[h] for h in heads]
    o_st = [_mm_nt(q_dec[:, hs[h]], st[h], HG_PASSES) for h in heads]
    upd = [_mm(v[:, hs[h]].T, k_end[:, hs[h]], HG_PASSES) for h in heads]
    for h in heads:
        st_ref[h] = st[h] * w_end[:, hs[h]] + upd[h]
    yield

    zg = g_ref[...]
    gate = zg * jax.nn.sigmoid(zg)
    ng = ng_ref[...]
    for h in heads:
        y = o[h] + o_st[h]
        y = y * lax.rsqrt(jnp.mean(y * y, axis=-1, keepdims=True) + EPS) * ng[:, hs[h]]
        o_ref[:, hs[h]] = (y * gate[:, hs[h]]).astype(o_ref.dtype)


def _rwkv_pre_kernel(zr_ref, zk_ref, zv_ref, zg_ref, zwa_ref,
                     mr_ref, mk_ref, mv_ref, mg_ref, mwa_ref,
                     w0_ref, w2_ref, a0_ref, a2_ref, g2_ref, kk_ref, ka_ref,
                     r_out, lw_out, k_out, v_out, kk_out, a_out, g_out,
                     er_ref, ek_ref, ev_ref, eg_ref, ewa_ref, *, tm, tp, pad, nb):
    i = pl.program_id(0)

    def shifted(z_ref, e_ref, mu_ref):
        @pl.when(i == 0)
        def _():
            e_ref[0:SUBLANES, :] = jnp.zeros((SUBLANES, e_ref.shape[1]), F32)

        x = z_ref[...]
        e_ref[SUBLANES:SUBLANES + tm, :] = x
        prev = e_ref[SUBLANES - 1:SUBLANES - 1 + tm, :]
        e_ref[0:SUBLANES, :] = x[tm - SUBLANES:tm, :]
        return x + mu_ref[...] * (prev - x)

    xr = shifted(zr_ref, er_ref, mr_ref)
    xk = shifted(zk_ref, ek_ref, mk_ref)
    xv = shifted(zv_ref, ev_ref, mv_ref)
    xg = shifted(zg_ref, eg_ref, mg_ref)
    xwa = shifted(zwa_ref, ewa_ref, mwa_ref)

    row = i * tm + lax.broadcasted_iota(jnp.int32, (tm, 1), 0)
    valid = _row_valid(row, tp, pad, nb)

    w = w0_ref[...] + jnp.dot(jnp.tanh(xwa), w2_ref[...], precision=HIGHEST, preferred_element_type=F32)
    w = -_softplus(-w) - 0.5
    lw_out[...] = -jnp.exp(w)
    a = jax.nn.sigmoid(a0_ref[...] + jnp.dot(xwa, a2_ref[...], precision=HIGHEST, preferred_element_type=F32))
    a_out[...] = a.astype(a_out.dtype)
    g_out[...] = jnp.dot(jax.nn.sigmoid(xg).astype(BF16), g2_ref[...],
                         preferred_element_type=F32).astype(g_out.dtype)
    r_out[...] = xr.astype(r_out.dtype)
    v_out[...] = jnp.where(valid, xv, 0.0).astype(v_out.dtype)
    k_out[...] = jnp.where(valid, xk * (1.0 + (a - 1.0) * ka_ref[...]), 0.0).astype(k_out.dtype)
    kk_out[...] = jnp.where(valid, xk * kk_ref[...], 0.0).astype(kk_out.dtype)


def _rwkv_pre(z, mu, w0, w2, a0, a2, g2, k_k, k_a, *, nb, tp, pad, tm):
    m = z.shape[0]
    mu_r, mu_k, mu_v = (mu[s:s + RW_WIDTH].reshape(1, RW_WIDTH) for s in (0, RW_WIDTH, 2 * RW_WIDTH))
    o = 3 * RW_WIDTH
    mu_wa = mu[o:o + 128].reshape(1, 128)
    mu_g = jnp.pad(mu[o + 128:o + 128 + RW_GATE_LORA], (0, 256 - RW_GATE_LORA)).reshape(1, 256)
    w2p = jnp.concatenate([w2, jnp.zeros_like(a2)], axis=0)
    a2p = jnp.concatenate([jnp.zeros_like(w2), a2], axis=0)
    g2p = jnp.pad(g2, ((0, 256 - RW_GATE_LORA), (0, 0)))
    kern = functools.partial(_rwkv_pre_kernel, tm=tm, tp=tp, pad=pad, nb=nb)

    def zspec(off, width):
        return pl.BlockSpec((tm, width), lambda i: (i, off // width))

    def full(shape):
        return pl.BlockSpec(shape, lambda i: (0,) * len(shape))

    vec = full((1, RW_WIDTH))
    wide = pl.BlockSpec((tm, RW_WIDTH), lambda i: (i, 0))
    outs = pl.pallas_call(
        kern,
        grid=(m // tm,),
        in_specs=[zspec(Z_R, RW_WIDTH), zspec(Z_K, RW_WIDTH), zspec(Z_V, RW_WIDTH), zspec(Z_GL, 256), zspec(Z_WA, 128),
                  vec, vec, vec, full((1, 256)), full((1, 128)),
                  vec, full((128, RW_WIDTH)), vec, full((128, RW_WIDTH)), full((256, RW_WIDTH)), vec, vec],
        out_specs=[wide] * 7,
        out_shape=[jax.ShapeDtypeStruct((m, RW_WIDTH), F32 if n == 1 else BF16) for n in range(7)],
        scratch_shapes=[pltpu.VMEM((tm + SUBLANES, RW_WIDTH), F32)] * 3
        + [pltpu.VMEM((tm + SUBLANES, 256), F32), pltpu.VMEM((tm + SUBLANES, 128), F32)],
        compiler_params=_params(("arbitrary",)),
        name="rwkv_pre",
    )(z, z, z, z, z, mu_r, mu_k, mu_v, mu_g, mu_wa,
      w0.reshape(1, -1), w2p, a0.reshape(1, -1), a2p, g2p.astype(BF16), k_k.reshape(1, -1), k_a.reshape(1, -1))
    return outs


_NN = (((1,), (0,)), ((), ()))
_NT = (((1,), (1,)), ((), ()))


def _split2(a):
    hi = a.astype(BF16)
    return hi, (a - hi.astype(F32)).astype(BF16)


def _dot(a, b, dims, passes):
    if passes == 6:
        return lax.dot_general(a, b, dims, precision=HIGHEST, preferred_element_type=F32)
    if passes == 1:
        return lax.dot_general(a.astype(BF16), b.astype(BF16), dims, preferred_element_type=F32)
    ah, al = _split2(a)
    bh, bl = _split2(b)
    out = lax.dot_general(ah, bh, dims, preferred_element_type=F32)
    out = out + lax.dot_general(ah, bl, dims, preferred_element_type=F32)
    return out + lax.dot_general(al, bh, dims, preferred_element_type=F32)


def _mm(a, b, passes=6):
    return _dot(a, b, _NN, passes)


def _mm_nt(a, b, passes=6):
    return _dot(a, b, _NT, passes)


def _bf(x):
    return x.astype(BF16)


def _bmm(a, b):
    return lax.dot_general(a, b, _NN, preferred_element_type=F32)


def _bmm_nt(a, b):
    return lax.dot_general(a, b, _NT, preferred_element_type=F32)


def _geometric_sums(xs, eye, steps):
    n = xs[0].shape[0]
    s = [eye + x for x in xs]
    if steps == 1:
        return s
    p = [_bmm(b, b) for b in map(_bf, xs)]
    yield
    for k in range(1, steps):
        pb = [_bf(pp) for pp in p]
        if k == steps - 1:
            s = [ss + _bmm(b, _bf(ss)) for ss, b in zip(s, pb)]
        else:
            both = [_bmm(b, jnp.concatenate([b, _bf(ss)], axis=1)) for b, ss in zip(pb, s)]
            p = [r[:, 0:n] for r in both]
            s = [ss + r[:, n:2 * n] for ss, r in zip(s, both)]
        yield
    return s


def _unit_lower_inverse(n_mats, blk):
    n = n_mats[0].shape[0]
    ri = lax.broadcasted_iota(jnp.int32, (n, n), 0)
    ci = lax.broadcasted_iota(jnp.int32, (n, n), 1)
    eye = (ri == ci).astype(F32)
    diag_blk = _block_id(ri, blk) == _block_id(ci, blk)
    d_inv = yield from _geometric_sums([jnp.where(diag_blk, -m, 0.0) for m in n_mats], eye, int(np.log2(blk)))
    d_inv_b = [_bf(d) for d in d_inv]
    m2 = [-_bmm(d, _bf(jnp.where(diag_blk, 0.0, m))) for d, m in zip(d_inv_b, n_mats)]
    yield
    t = yield from _geometric_sums(m2, eye, int(np.log2(RW_CHUNK // blk)))
    t_inv = [_bmm(_bf(tt), d) for tt, d in zip(t, d_inv_b)]
    yield
    return t_inv


def _rwkv_stages(r_ref, lw_ref, k_ref, v_ref, kk_ref, a_ref, g_ref, rk_ref, lng_ref, lnb_ref,
                 o_ref, st_ref, *, chunk, npair):
    @pl.when(pl.program_id(1) == 0)
    def _():
        st_ref[...] = jnp.zeros_like(st_ref)

    L = chunk
    n2 = 2 * L
    lane = lax.broadcasted_iota(jnp.int32, (1, LANES), 1)
    m0 = (lane < RW_HEAD).astype(F32)
    m1 = 1.0 - m0
    tri = (lax.broadcasted_iota(jnp.int32, (L, L), 0) >= lax.broadcasted_iota(jnp.int32, (L, L), 1)).astype(F32)
    ri = lax.broadcasted_iota(jnp.int32, (n2, n2), 0)
    ci = lax.broadcasted_iota(jnp.int32, (n2, n2), 1)
    strict = ri > ci
    incl = ri >= ci
    cum_all = _mm(tri, lw_ref[...])

    def stack(x):
        return jnp.concatenate([x * m0, x * m1], axis=0)

    def head_mean(x):
        s0 = jnp.sum(x * m0, axis=-1, keepdims=True)
        s1 = jnp.sum(x * m1, axis=-1, keepdims=True)
        return (s0 * m0 + s1 * m1) * (1.0 / RW_HEAD)

    pairs = range(npair)
    sls = [slice(p * LANES, (p + 1) * LANES) for p in pairs]
    ops = []
    for sl in sls:
        r = r_ref[:, sl].astype(F32)
        lw = lw_ref[:, sl]
        k = k_ref[:, sl].astype(F32)
        kk = kk_ref[:, sl].astype(F32)
        kn = kk / jnp.maximum(jnp.sqrt(head_mean(kk * kk) * RW_HEAD + 1e-24), 1e-12)
        cum = cum_all[:, sl]
        c_last = cum[L - 1:L, :]
        e_neg = jnp.exp(-cum)
        e_end = jnp.exp(c_last - cum)
        beta = kn * a_ref[:, sl].astype(F32)
        v_s = stack(v_ref[:, sl].astype(F32))
        ops.append(dict(
            al_t=_bf(stack(kn * jnp.exp(cum - lw))),
            r_t=_bf(stack(r * jnp.exp(cum))),
            be_h=_bf(stack(beta * e_neg)),
            k_h=_bf(stack(k * e_neg)),
            be_e=_bf(stack(beta * e_end)),
            k_e=_bf(stack(k * e_end)),
            v_s=v_s, v_b=_bf(v_s),
            w_end=jnp.exp(c_last)))
    yield

    sc = [_bmm_nt(jnp.concatenate([o["al_t"], o["r_t"]], axis=0), jnp.concatenate([o["be_h"], o["k_h"]], axis=0))
          for o in ops]
    a_ab = [jnp.where(strict, s[0:n2, 0:n2], 0.0) for s in sc]
    a_ak = [_bf(jnp.where(strict, s[0:n2, n2:2 * n2], 0.0)) for s in sc]
    a_rb = [_bf(jnp.where(incl, s[n2:2 * n2, 0:n2], 0.0)) for s in sc]
    a_rk = [_bf(jnp.where(incl, s[n2:2 * n2, n2:2 * n2], 0.0)) for s in sc]
    gy = [_bmm(jnp.concatenate([ak, rk], axis=0), o["v_b"]) for ak, rk, o in zip(a_ak, a_rk, ops)]
    yield
    t_inv = yield from _unit_lower_inverse(a_ab, 16)
    hats = [_bmm(_bf(t), jnp.concatenate([o["al_t"], _bf(g[0:n2])], axis=1)) for t, o, g in zip(t_inv, ops, gy)]
    yield

    st = [st_ref[p] for p in pairs]
    from_st = [_bmm_nt(jnp.concatenate([_bf(h[:, 0:LANES]), o["r_t"]], axis=0), _bf(s))
               for h, o, s in zip(hats, ops, st)]
    u = [-(f[0:n2] + h[:, LANES:2 * LANES]) for f, h in zip(from_st, hats)]
    yield
    y_u = [_bmm(a, _bf(uu)) for a, uu in zip(a_rb, u)]
    upd = [_bmm(_bf(jnp.concatenate([uu, o["v_s"]], axis=0).T), jnp.concatenate([o["be_e"], o["k_e"]], axis=0))
           for uu, o in zip(u, ops)]
    for p in pairs:
        st_ref[p] = st[p] * ops[p]["w_end"] + upd[p]
    yield
    y_st = [f[n2:2 * n2] for f in from_st]
    y_loc = [g[n2:2 * n2] for g in gy]

    for p, sl in enumerate(sls):
        y = y_st[p] + y_u[p] + y_loc[p]
        y = y[0:L] + y[L:n2]
        r = r_ref[:, sl].astype(F32)
        k = k_ref[:, sl].astype(F32)
        v = v_ref[:, sl].astype(F32)
        mu = head_mean(y)
        yc = y - mu
        var = head_mean(yc * yc)
        yn = yc * lax.rsqrt(var + RW_GN_EPS) * lng_ref[:, sl] + lnb_ref[:, sl]
        bonus = head_mean(r * k * rk_ref[:, sl]) * RW_HEAD
        o_ref[:, sl] = ((yn + bonus * v) * g_ref[:, sl]).astype(o_ref.dtype)


def _recurrences_kernel(hq_ref, hf_ref, hi_ref, hg_ref, lb_ref, ng_ref,
                        r_ref, lw_ref, k_ref, v_ref, kk_ref, a_ref, g_ref, rk_ref, lng_ref, lnb_ref,
                        ya_ref, yb_ref, hst_ref, rst_ref, *, pad):
    stages = [_rwkv_stages(r_ref, lw_ref, k_ref, v_ref, kk_ref, a_ref, g_ref, rk_ref, lng_ref, lnb_ref,
                           yb_ref, rst_ref, chunk=RW_CHUNK, npair=RW_WIDTH // LANES),
              _hgrn_stages(hq_ref, hf_ref, hi_ref, hg_ref, lb_ref, ng_ref, ya_ref, hst_ref,
                           chunk=HG_CHUNK, pad=pad, nheads=HG_HEADS)]
    while stages:
        for gen in list(stages):
            if next(gen, stages) is stages:
                stages.remove(gen)


def _recurrences(z, lb, norm_g, pre, r_k, ln_g, ln_b, *, nb, tp, pad):
    assert HG_CHUNK == RW_CHUNK and HG_WIDTH == RW_WIDTH
    m = z.shape[0]
    L = HG_CHUNK
    nc = tp // L

    def zspec(off):
        return pl.BlockSpec((L, HG_WIDTH), lambda b, c: (b * nc + c, off // HG_WIDTH))

    blk = pl.BlockSpec((L, RW_WIDTH), lambda b, c: (b * nc + c, 0))
    pspec = pl.BlockSpec((1, RW_WIDTH), lambda b, c: (0, 0))
    row = lambda x: x.reshape(1, -1).astype(F32)
    return pl.pallas_call(
        functools.partial(_recurrences_kernel, pad=pad),
        grid=(nb, nc),
        in_specs=[zspec(Z_HQ), zspec(Z_HF), zspec(Z_HI), zspec(Z_HG), pspec, pspec] + [blk] * 7 + [pspec] * 3,
        out_specs=[blk, blk],
        out_shape=[jax.ShapeDtypeStruct((m, HG_WIDTH), BF16), jax.ShapeDtypeStruct((m, RW_WIDTH), BF16)],
        scratch_shapes=[pltpu.VMEM((HG_HEADS, HG_DIM, HG_DIM), F32),
                        pltpu.VMEM((RW_WIDTH // LANES, LANES, LANES), F32)],
        compiler_params=_params(("arbitrary", "arbitrary")),
        name="recurrences",
    )(z, z, z, z, row(lb), row(norm_g), *pre, row(r_k), row(ln_g), row(ln_b))


def _rope_kernel(x_ref, cos_ref, sin_ref, o_ref, *, n_plain, n_rope, scale, is_query, tr, nrb, pad):
    lane = lax.broadcasted_iota(jnp.int32, (1, LANES), 1)
    cos = cos_ref[...]
    sin = sin_ref[...]
    if is_query:
        bias = jnp.ones((tr, 1), F32)
    else:
        pos = (pl.program_id(0) % nrb) * tr + lax.broadcasted_iota(jnp.int32, (tr, 1), 0)
        bias = jnp.where(pos < pad, NEG_BIG, 0.0)
    if n_plain:
        o_ref[:, 0:n_plain * LANES] = (x_ref[:, 0:n_plain * LANES] * scale).astype(o_ref.dtype)
    for h in range(n_plain, n_plain + n_rope):
        x = x_ref[:, h * LANES:(h + 1) * LANES]
        rot = jnp.where(lane < MLA_ROPE // 2, -pltpu.roll(x, LANES - MLA_ROPE // 2, 1),
                        pltpu.roll(x, MLA_ROPE // 2, 1))
        y = (x * cos + rot * sin) * scale
        o_ref[:, h * LANES:(h + 1) * LANES] = jnp.where(lane == MLA_ROPE, bias, y).astype(o_ref.dtype)


def _rope(x, col_block, n_plain, n_rope, cos, sin, scale, is_query, *, tp, tr, pad):
    m = x.shape[0]
    nrb = tp // tr
    width = (n_plain + n_rope) * LANES
    kern = functools.partial(_rope_kernel, n_plain=n_plain, n_rope=n_rope, scale=scale, is_query=is_query,
                             tr=tr, nrb=nrb, pad=pad)
    tab = pl.BlockSpec((tr, LANES), lambda i: (i % nrb, 0))
    return pl.pallas_call(
        kern,
        grid=(m // tr,),
        in_specs=[pl.BlockSpec((tr, width), lambda i: (i, col_block)), tab, tab],
        out_specs=pl.BlockSpec((tr, width), lambda i: (i, 0)),
        out_shape=jax.ShapeDtypeStruct((m, width), BF16),
        compiler_params=_params(("arbitrary",)),
        name="rope",
    )(x, cos, sin)


def _q_proj_kernel(x_ref, g_ref, w_ref, cos_ref, sin_ref, o_ref, xn_ref, *, n_plain, scale, eps):
    j = pl.program_id(1)

    @pl.when(j == 0)
    def _():
        xn_ref[...] = _normed_rows(x_ref[...], g_ref[...], eps, 0, 0, 0, 0).astype(BF16)

    y = jnp.dot(xn_ref[...], w_ref[...], preferred_element_type=F32)

    @pl.when(j < n_plain)
    def _():
        o_ref[...] = (y * scale).astype(o_ref.dtype)

    @pl.when(j >= n_plain)
    def _():
        lane = lax.broadcasted_iota(jnp.int32, (1, LANES), 1)
        cos = cos_ref[...]
        sin = sin_ref[...]
        for b in range(y.shape[1] // LANES):
            x = y[:, b * LANES:(b + 1) * LANES]
            rot = jnp.where(lane < MLA_ROPE // 2, -pltpu.roll(x, LANES - MLA_ROPE // 2, 1),
                            pltpu.roll(x, MLA_ROPE // 2, 1))
            r = (x * cos + rot * sin) * scale
            o_ref[:, b * LANES:(b + 1) * LANES] = jnp.where(lane == MLA_ROPE, 1.0, r).astype(o_ref.dtype)


def _q_proj(z, g, w, cos_rows, sin_rows, scale, *, tm, tn):
    m = z.shape[0]
    k = MLA_Q_RANK
    n = w.shape[1]
    kern = functools.partial(_q_proj_kernel, n_plain=MLA_HEADS * MLA_NOPE // tn, scale=scale, eps=EPS)
    tab = pl.BlockSpec((tm, LANES), lambda i, j: (i, 0))
    return pl.pallas_call(
        kern,
        grid=(m // tm, n // tn),
        in_specs=[pl.BlockSpec((tm, k), lambda i, j: (i, Z_CQ // k)),
                  pl.BlockSpec((1, k), lambda i, j: (0, 0)),
                  pl.BlockSpec((k, tn), lambda i, j: (0, j)), tab, tab],
        out_specs=pl.BlockSpec((tm, tn), lambda i, j: (i, j)),
        out_shape=jax.ShapeDtypeStruct((m, n), BF16),
        scratch_shapes=[pltpu.VMEM((tm, k), BF16)],
        compiler_params=_params(("arbitrary", "arbitrary")),
        name="q_proj",
    )(z, g.reshape(1, k).astype(F32), w, cos_rows, sin_rows)


def _attn_kernel(qn_ref, qr_ref, kn_ref, kr_ref, vt_ref, bias_ref, o_ref,
                 s0_ref, s1_ref, m_ref, l_ref, acc_ref, *, t, nq):
    qi = pl.program_id(2)
    q = jnp.concatenate([qn_ref[...], qr_ref[...]], axis=1)

    def scores_t(off):
        kcat = jnp.concatenate([kn_ref[pl.ds(off, 4 * t), :], kr_ref[pl.ds(off, 4 * t), :]], axis=1)
        return lax.dot_general(kcat, q, _NT, preferred_element_type=F32)

    def first_tile(jq):
        return jnp.minimum(4 * jq, nq - 4)

    def issue_scores(jq, dst_ref):
        first = first_tile(jq)
        s = scores_t(pl.multiple_of(first * t, t))
        for n in range(4):
            x = first + n
            kind = jnp.where(x < 4 * jq, 2, jnp.where(x < qi, 0, jnp.where(x == qi, 1, 2)))
            dst_ref[n * t:(n + 1) * t, :] = s[n * t:(n + 1) * t] + bias_ref[kind]

    def consume(jq, src_ref):
        first = first_tile(jq)
        for stream in range(2):
            s = src_ref[2 * stream * t:2 * (stream + 1) * t, :]
            m_i = m_ref[stream]
            m_new = jnp.maximum(m_i, jnp.max(s, axis=0, keepdims=True))
            alpha = jnp.exp2(m_i - m_new)
            p = jnp.exp2(s - m_new)
            m_ref[stream] = m_new
            l_ref[stream] = alpha * l_ref[stream] + jnp.sum(p, axis=0, keepdims=True)
            v_t = jnp.concatenate([vt_ref[0, first + 2 * stream], vt_ref[0, first + 2 * stream + 1]], axis=1)
            acc_ref[stream] = alpha * acc_ref[stream] + jnp.dot(v_t, p.astype(BF16), preferred_element_type=F32)

    def half(jq, cur_ref, nxt_ref):
        issue_scores(jq + 1, nxt_ref)
        consume(jq, cur_ref)

    last = qi // 4
    m_ref[...] = jnp.full(m_ref.shape, NEG_BIG, F32)
    l_ref[...] = jnp.zeros(l_ref.shape, F32)
    acc_ref[...] = jnp.zeros(acc_ref.shape, F32)
    issue_scores(0, s0_ref)

    def trip(r, carry):
        half(2 * r, s0_ref, s1_ref)

        @pl.when(2 * r + 1 < last)
        def _():
            half(2 * r + 1, s1_ref, s0_ref)

        return carry

    lax.fori_loop(0, (last + 1) // 2, trip, 0)

    @pl.when(last % 2 == 0)
    def _():
        consume(last, s0_ref)

    @pl.when(last % 2 == 1)
    def _():
        consume(last, s1_ref)

    m_a, m_b = m_ref[0], m_ref[1]
    m_i = jnp.maximum(m_a, m_b)
    w_a = jnp.exp2(m_a - m_i)
    w_b = jnp.exp2(m_b - m_i)
    out_t = (w_a * acc_ref[0] + w_b * acc_ref[1]) / (w_a * l_ref[0] + w_b * l_ref[1])
    o_ref[...] = out_t.T.astype(o_ref.dtype)


def _attention(q, kr, kv, *, nb, tp):
    m = q.shape[0]
    tq = _pick(tp, tuple(c for c in (384, 256, 128) if tp >= 4 * c))
    nq = tp // tq
    kern = functools.partial(_attn_kernel, t=tq, nq=nq)
    allowed_t = jnp.arange(tq)[:, None] <= jnp.arange(tq)[None, :]
    bias = jnp.stack([jnp.zeros((tq, tq), F32), jnp.where(allowed_t, 0.0, NEG_BIG).astype(F32),
                      jnp.full((tq, tq), NEG_BIG, F32)])
    v_t = kv[:, MLA_HEADS * MLA_NOPE:].reshape(nb, nq, tq, MLA_HEADS, MLA_V)
    v_t = v_t.transpose(0, 3, 1, 4, 2).reshape(nb * MLA_HEADS, nq, MLA_V, tq)
    return pl.pallas_call(
        kern,
        grid=(nb, MLA_HEADS, nq),
        in_specs=[pl.BlockSpec((tq, LANES), lambda b, h, i: (b * nq + i, h)),
                  pl.BlockSpec((tq, LANES), lambda b, h, i: (b * nq + i, MLA_HEADS + h)),
                  pl.BlockSpec((tp, LANES), lambda b, h, i: (b, h)),
                  pl.BlockSpec((tp, LANES), lambda b, h, i: (b, 0)),
                  pl.BlockSpec((1, nq, MLA_V, tq), lambda b, h, i: (b * MLA_HEADS + h, 0, 0, 0)),
                  pl.BlockSpec((3, tq, tq), lambda b, h, i: (0, 0, 0))],
        out_specs=pl.BlockSpec((tq, LANES), lambda b, h, i: (b * nq + i, h)),
        out_shape=jax.ShapeDtypeStruct((m, MLA_WIDTH), BF16),
        scratch_shapes=[pltpu.VMEM((4 * tq, tq), F32), pltpu.VMEM((4 * tq, tq), F32),
                        pltpu.VMEM((2, 1, tq), F32), pltpu.VMEM((2, 1, tq), F32), pltpu.VMEM((2, MLA_V, tq), F32)],
        compiler_params=_params(("arbitrary", "arbitrary", "arbitrary")),
        name="mla_attention",
    )(q, q, kv, kr, v_t, bias)


def _merge_kernel(ya_ref, yb_ref, yc_ref, ga_ref, gb_ref, gc_ref, w_ref, o_ref):
    acc = None
    for n, (y_ref, g_ref) in enumerate(((ya_ref, ga_ref), (yb_ref, gb_ref), (yc_ref, gc_ref))):
        p = jnp.dot(y_ref[...], w_ref[n], preferred_element_type=F32)
        t = jax.nn.sigmoid(g_ref[...]) * p
        acc = t if acc is None else acc + t
    o_ref[...] = acc.astype(o_ref.dtype)


def _merge(ya, yb, yc, z, w_branch, *, tm, tn):
    m = ya.shape[0]
    d = w_branch.shape[2]
    yspec = pl.BlockSpec((tm, HG_WIDTH), lambda i, j: (i, 0))

    def gspec(n):
        return pl.BlockSpec((tm, tn), lambda i, j: (i, (Z_GATES + n * d) // tn + j))

    return pl.pallas_call(
        _merge_kernel,
        grid=(m // tm, d // tn),
        in_specs=[yspec, yspec, yspec, gspec(0), gspec(1), gspec(2),
                  pl.BlockSpec((N_BRANCH, HG_WIDTH, tn), lambda i, j: (0, 0, j))],
        out_specs=pl.BlockSpec((tm, tn), lambda i, j: (i, j)),
        out_shape=jax.ShapeDtypeStruct((m, d), BF16),
        compiler_params=_params(("arbitrary", "arbitrary")),
        name="merge",
    )(ya, yb, yc, z, z, z, w_branch)


def _matmul_res_kernel(x_ref, w_ref, r_ref, o_ref):
    o_ref[...] = r_ref[...] + jnp.dot(x_ref[...], w_ref[...], preferred_element_type=F32)


def _matmul_res(x, w, res, *, tm, tn):
    m, k = x.shape
    n = w.shape[1]
    return pl.pallas_call(
        _matmul_res_kernel,
        grid=(m // tm, n // tn),
        in_specs=[pl.BlockSpec((tm, k), lambda i, j: (i, 0)),
                  pl.BlockSpec((k, tn), lambda i, j: (0, j)),
                  pl.BlockSpec((tm, tn), lambda i, j: (i, j))],
        out_specs=pl.BlockSpec((tm, tn), lambda i, j: (i, j)),
        out_shape=jax.ShapeDtypeStruct((m, n), F32),
        compiler_params=_params(("arbitrary", "arbitrary")),
        name="matmul_res",
    )(x, w, res)


def _ffn_up_kernel(x_ref, g_ref, wg_ref, wv_ref, cwg_ref, cwv_ref, cbg_ref, cbv_ref, o_ref,
                   xn_ref, eg_ref, ev_ref, pg_ref, pv_ref, *, tm, tp, pad, nb, eps):
    i = pl.program_id(0)
    j = pl.program_id(1)

    @pl.when(j == 0)
    def _():
        xn_ref[...] = _normed_rows(x_ref[...], g_ref[...], eps, i * tm, tp, pad, nb).astype(BF16)

    @pl.when(i == 0)
    def _():
        pg_ref[j] = jnp.zeros(pg_ref.shape[1:], F32)
        pv_ref[j] = jnp.zeros(pv_ref.shape[1:], F32)

    def conv(w_ref, e_ref, p_ref, cw_ref, cb_ref):
        u = jnp.dot(xn_ref[...], w_ref[...], preferred_element_type=F32)
        e_ref[0:SUBLANES, :] = p_ref[j]
        e_ref[SUBLANES:SUBLANES + tm, :] = u
        p_ref[j] = u[tm - SUBLANES:tm, :]
        u1 = e_ref[SUBLANES - 1:SUBLANES - 1 + tm, :]
        u2 = e_ref[SUBLANES - 2:SUBLANES - 2 + tm, :]
        cw = cw_ref[...]
        return cw[0:1, :] * u2 + cw[1:2, :] * u1 + cw[2:3, :] * u + cb_ref[...]

    gate = conv(wg_ref, eg_ref, pg_ref, cwg_ref, cbg_ref)
    val = conv(wv_ref, ev_ref, pv_ref, cwv_ref, cbv_ref)
    o_ref[...] = (gate * jax.nn.sigmoid(gate) * val).astype(o_ref.dtype)


def _ffn_up(h, g, w_up, conv_w, conv_b, *, tp, pad, nb, tm, tf):
    m, d = h.shape
    dff = w_up.shape[1] // 2
    nj = dff // tf
    kern = functools.partial(_ffn_up_kernel, tm=tm, tp=tp, pad=pad, nb=nb, eps=EPS)
    cb = conv_b.reshape(1, -1).astype(F32)
    return pl.pallas_call(
        kern,
        grid=(m // tm, nj),
        in_specs=[pl.BlockSpec((tm, d), lambda i, j: (i, 0)),
                  pl.BlockSpec((1, d), lambda i, j: (0, 0)),
                  pl.BlockSpec((d, tf), lambda i, j: (0, j)),
                  pl.BlockSpec((d, tf), lambda i, j: (0, nj + j)),
                  pl.BlockSpec((CONV_W, tf), lambda i, j: (0, j)),
                  pl.BlockSpec((CONV_W, tf), lambda i, j: (0, nj + j)),
                  pl.BlockSpec((1, tf), lambda i, j: (0, j)),
                  pl.BlockSpec((1, tf), lambda i, j: (0, nj + j))],
        out_specs=pl.BlockSpec((tm, tf), lambda i, j: (i, j)),
        out_shape=jax.ShapeDtypeStruct((m, dff), BF16),
        scratch_shapes=[pltpu.VMEM((tm, d), BF16),
                        pltpu.VMEM((tm + SUBLANES, tf), F32), pltpu.VMEM((tm + SUBLANES, tf), F32),
                        pltpu.VMEM((nj, SUBLANES, tf), F32), pltpu.VMEM((nj, SUBLANES, tf), F32)],
        compiler_params=_params(("arbitrary", "arbitrary")),
        name="ffn_up",
    )(h, g.reshape(1, d).astype(F32), w_up, w_up, conv_w.astype(F32), conv_w.astype(F32), cb, cb)


def _final_norm_kernel(x_ref, g_ref, o_ref):
    x = x_ref[0]
    o_ref[0] = x * lax.rsqrt(jnp.mean(x * x, axis=-1, keepdims=True) + EPS) * g_ref[...]


def _final_norm(h3, g, *, skip, seq):
    nb, tp, d = h3.shape
    tr = LANES
    return pl.pallas_call(
        _final_norm_kernel,
        grid=(nb, seq // tr),
        in_specs=[pl.BlockSpec((1, tr, d), lambda b, i: (b, skip // tr + i, 0)),
                  pl.BlockSpec((1, d), lambda b, i: (0, 0))],
        out_specs=pl.BlockSpec((1, tr, d), lambda b, i: (b, i, 0)),
        out_shape=jax.ShapeDtypeStruct((nb, seq, d), F32),
        compiler_params=_params(("arbitrary", "arbitrary")),
        name="final_norm",
    )(h3, g.reshape(1, d).astype(F32))


def _layout_w_in(w):
    d = w.shape[0]
    o = 0
    parts = {}
    for name, width in (("hg4", 4 * HG_WIDTH), ("rkv", 3 * RW_WIDTH), ("wl", RW_DECAY_LORA), ("al", RW_AAA_LORA),
                        ("gl", RW_GATE_LORA), ("cq", MLA_Q_RANK), ("ckv", MLA_KV_RANK), ("kr", MLA_ROPE),
                        ("gates", N_BRANCH * d)):
        parts[name] = w[:, o:o + width]
        o += width
    zeros = lambda n: jnp.zeros((d, n), w.dtype)
    cols = [parts["hg4"], parts["rkv"], parts["cq"], parts["ckv"],
            parts["gl"], zeros(256 - RW_GATE_LORA), parts["wl"], parts["al"],
            parts["kr"], zeros(LANES - MLA_ROPE), parts["gates"]]
    return jnp.concatenate(cols, axis=1).astype(BF16)


def _layout_w_uq(w):
    r = w.shape[0]
    w = w.reshape(r, MLA_HEADS, MLA_NOPE + MLA_ROPE)
    nope = w[:, :, :MLA_NOPE].reshape(r, MLA_HEADS * MLA_NOPE)
    rope = jnp.pad(w[:, :, MLA_NOPE:], ((0, 0), (0, 0), (0, LANES - MLA_ROPE))).reshape(r, MLA_HEADS * LANES)
    return jnp.concatenate([nope, rope], axis=1).astype(BF16)


def _layout_w_ukv(w):
    r = w.shape[0]
    w = w.reshape(r, MLA_HEADS, MLA_NOPE + MLA_V)
    return jnp.concatenate([w[:, :, :MLA_NOPE].reshape(r, -1), w[:, :, MLA_NOPE:].reshape(r, -1)], axis=1).astype(BF16)


def _rope_tables(tp, pad):
    inv = ROPE_BASE ** (-jnp.arange(0, MLA_ROPE, 2, dtype=F32) / MLA_ROPE)
    pos = jnp.arange(tp, dtype=F32) - float(pad)
    ang = pos[:, None] * inv[None, :]
    z = jnp.zeros((tp, LANES - MLA_ROPE), F32)
    cos = jnp.concatenate([jnp.cos(ang), jnp.cos(ang), z], axis=1)
    sin = jnp.concatenate([jnp.sin(ang), jnp.sin(ang), z], axis=1)
    return cos, sin


def kernel(x, meta_tokens, norm1_g, w_in, hg_lb_logits, hg_norm_g, rw_mu, rw_w0, rw_w2, rw_a0, rw_a2, rw_g2, rw_k_k, rw_k_a, rw_r_k, rw_ln_g, rw_ln_b, mla_q_norm_g, mla_w_uq, mla_kv_norm_g, mla_w_ukv, w_branch, w_out, norm2_g, ffn_w_up, ffn_conv_w, ffn_conv_b, ffn_w_down, final_norm_g):
    nb, seq, d = x.shape
    depth = w_in.shape[0]
    t = N_META + seq
    pad = (-N_META) % LANES
    tp = pad + t
    m = nb * tp
    tm = _pick(m, (768, 384, 256, 128))
    tm_wide = _pick(m, (1376, 768, 384, 256, 128))
    geo = dict(tp=tp, pad=pad, nb=nb)

    meta = jnp.broadcast_to(meta_tokens[None].astype(x.dtype), (nb, N_META, d))
    h = jnp.concatenate([jnp.zeros((nb, pad, d), x.dtype), meta, x], axis=1).reshape(m, d)
    cos, sin = _rope_tables(tp, pad)
    cos_rows, sin_rows = jnp.tile(cos, (nb, 1)), jnp.tile(sin, (nb, 1))
    p_lb = jax.nn.softmax(hg_lb_logits.astype(F32), axis=0)
    lower_bounds = jnp.cumsum(p_lb, axis=0) - p_lb[0]
    scale = float((MLA_NOPE + MLA_ROPE) ** -0.5 * np.log2(np.e))

    for l in range(depth):
        z = _norm_matmul(h, 0, d, norm1_g[l], _layout_w_in(w_in[l]), F32, tm=tm_wide, tn=512, **geo)
        pre = _rwkv_pre(z, rw_mu[l], rw_w0[l], rw_w2[l], rw_a0[l], rw_a2[l], rw_g2[l], rw_k_k[l], rw_k_a[l],
                        tm=_pick(m, (384, 256, 128)), **geo)
        y_a, y_b = _recurrences(z, lower_bounds[l], hg_norm_g[l], pre, rw_r_k[l], rw_ln_g[l], rw_ln_b[l], **geo)
        q_cat = _q_proj(z, mla_q_norm_g[l], _layout_w_uq(mla_w_uq[l]), cos_rows, sin_rows, scale, tm=tm, tn=512)
        kv = _norm_matmul(z, Z_CKV // MLA_KV_RANK, MLA_KV_RANK, mla_kv_norm_g[l], _layout_w_ukv(mla_w_ukv[l]), BF16,
                          tm=tm, tn=512, tp=tp, pad=0, nb=nb)
        tr = _pick(tp, (384, 256, 128))
        k_rope = _rope(z, Z_KR // LANES, 0, 1, cos, sin, 1.0, False, tp=tp, tr=tr, pad=pad)
        y_c = _attention(q_cat, k_rope, kv, nb=nb, tp=tp)
        merged = _merge(y_a, y_b, y_c, z, w_branch[l].astype(BF16), tm=tm, tn=512)
        h = _matmul_res(merged, w_out[l].astype(BF16), h, tm=tm, tn=512)
        act = _ffn_up(h, norm2_g[l], ffn_w_up[l].astype(BF16), ffn_conv_w[l], ffn_conv_b[l], tm=tm, tf=512, **geo)
        h = _matmul_res(act, ffn_w_down[l].astype(BF16), h, tm=tm, tn=512)
    return _final_norm(h.reshape(nb, tp, d), final_norm_g, skip=pad + N_META, seq=seq)
```

```python
import functools

import numpy as np
import jax
import jax.numpy as jnp
from jax import lax
from jax.experimental import pallas as pl
from jax.experimental.pallas import tpu as pltpu

F32 = jnp.float32
BF16 = jnp.bfloat16
HIGHEST = lax.Precision.HIGHEST

N_META = 16
EPS = 1e-6
NEG_BIG = -1e30
F_FLOOR = 1e-30
HG_HEADS = 8
HG_DIM = 128
HG_WIDTH = HG_HEADS * HG_DIM
RW_HEAD = 64
RW_HEADS = 16
RW_WIDTH = RW_HEADS * RW_HEAD
RW_DECAY_LORA = 64
RW_AAA_LORA = 64
RW_GATE_LORA = 160
RW_GN_EPS = 64e-5
MLA_HEADS = 8
MLA_Q_RANK = 512
MLA_KV_RANK = 512
MLA_NOPE = 128
MLA_ROPE = 64
MLA_V = 128
MLA_WIDTH = MLA_HEADS * MLA_V
ROPE_BASE = 10000.0
CONV_W = 3
N_BRANCH = 3

LANES = 128
SUBLANES = 8
VMEM_LIMIT = 56 * 1024 * 1024

Z_HQ, Z_HF, Z_HI, Z_HG = 0, 1024, 2048, 3072
Z_R, Z_K, Z_V = 4096, 5120, 6144
Z_CQ, Z_CKV = 7168, 7680
Z_GL = 8192
Z_WA = 8448
Z_KR = 8576
Z_GATES = 8704
Z_COLS = Z_GATES + N_BRANCH * 2048

RW_CHUNK = 64
HG_CHUNK = 64
HG_BASE = 8
HG_PASSES = 1


def _params(sem):
    return pltpu.CompilerParams(dimension_semantics=sem, vmem_limit_bytes=VMEM_LIMIT)


def _pick(n, cands):
    for c in cands:
        if n % c == 0:
            return c
    raise ValueError(f"no tile for {n} in {cands}")


def _block_id(idx, size):
    return lax.shift_right_logical(idx, int(np.log2(size)))


def _softplus(x):
    return jnp.maximum(x, 0.0) + jnp.log(1.0 + jnp.exp(-jnp.abs(x)))


def _row_valid(row, tp, pad, nb):
    valid = None
    for b in range(nb):
        ok = jnp.logical_or(row < b * tp, row >= b * tp + pad)
        valid = ok if valid is None else jnp.logical_and(valid, ok)
    return valid


def _normed_rows(x, g, eps, row0, tp, pad, nb):
    ms = jnp.mean(x * x, axis=-1, keepdims=True)
    y = x * lax.rsqrt(ms + eps) * g
    if pad:
        row = row0 + lax.broadcasted_iota(jnp.int32, (x.shape[0], 1), 0)
        y = jnp.where(_row_valid(row, tp, pad, nb), y, 0.0)
    return y


def _norm_matmul_kernel(x_ref, g_ref, w_ref, o_ref, xn_ref, *, tm, tp, pad, nb, eps):
    i = pl.program_id(0)

    @pl.when(pl.program_id(1) == 0)
    def _():
        xn_ref[...] = _normed_rows(x_ref[...], g_ref[...], eps, i * tm, tp, pad, nb).astype(BF16)

    o_ref[...] = jnp.dot(xn_ref[...], w_ref[...], preferred_element_type=F32).astype(o_ref.dtype)


def _norm_matmul(x, col_block, k, g, w, out_dtype, *, tp, pad, nb, tm, tn):
    m = x.shape[0]
    n = w.shape[1]
    kern = functools.partial(_norm_matmul_kernel, tm=tm, tp=tp, pad=pad, nb=nb, eps=EPS)
    return pl.pallas_call(
        kern,
        grid=(m // tm, n // tn),
        in_specs=[pl.BlockSpec((tm, k), lambda i, j: (i, col_block)),
                  pl.BlockSpec((1, k), lambda i, j: (0, 0)),
                  pl.BlockSpec((k, tn), lambda i, j: (0, j))],
        out_specs=pl.BlockSpec((tm, tn), lambda i, j: (i, j)),
        out_shape=jax.ShapeDtypeStruct((m, n), out_dtype),
        scratch_shapes=[pltpu.VMEM((tm, k), BF16)],
        compiler_params=_params(("arbitrary", "arbitrary")),
        name="norm_matmul",
    )(x, g.reshape(1, k).astype(F32), w)


def _hgrn_stages(q_ref, f_ref, i_ref, g_ref, lb_ref, ng_ref, o_ref, st_ref, *, chunk, pad, nheads):
    c = pl.program_id(1)

    @pl.when(c == 0)
    def _():
        st_ref[...] = jnp.zeros_like(st_ref)

    L = chunk
    base = HG_BASE
    heads = range(nheads)
    hs = [slice(h * HG_DIM, (h + 1) * HG_DIM) for h in heads]
    q = q_ref[...]
    zf = f_ref[...]
    v = i_ref[...]
    lb = lb_ref[...]
    row = lax.broadcasted_iota(jnp.int32, (L, 1), 0)
    valid = c * L + row >= pad
    f = lb + (1.0 - lb) * jax.nn.sigmoid(zf)
    logf = jnp.where(valid, jnp.log(jnp.maximum(f, F_FLOOR)), 0.0)
    k = jnp.where(valid, (1.0 - lb) * jax.nn.sigmoid(-zf), 0.0)
    tri = (row >= lax.broadcasted_iota(jnp.int32, (L, L), 1)).astype(F32)
    b = _mm(tri, logf)

    r = row & (base - 1)
    o = [jnp.zeros((L, HG_DIM), F32) for _ in heads]
    for lag in range(base):
        ks, bs, vs = (k, b, v) if lag == 0 else tuple(pltpu.roll(x, lag, 0) for x in (k, b, v))
        ok = r >= lag
        term = q * ks * jnp.exp(jnp.where(ok, b - bs, 0.0))
        for h in heads:
            a = jnp.sum(term[:, hs[h]], axis=-1, keepdims=True)
            o[h] = o[h] + jnp.where(ok, a, 0.0) * vs[:, hs[h]]
        yield

    size = L
    while size > base:
        half = size // 2
        blocks = []
        for lo in range(0, L, size):
            mid, hi = lo + half, lo + size
            b_mid = b[mid - 1:mid, :]
            blocks.append((lo, mid, q[mid:hi] * jnp.exp(b[mid:hi] - b_mid), k[lo:mid] * jnp.exp(b_mid - b[lo:mid])))
        sc = [[_mm_nt(qg[:, hs[h]], kg[:, hs[h]], HG_PASSES) for h in heads] for (_, _, qg, kg) in blocks]
        pv = [[_mm(sc[bi][h], v[lo:mid, hs[h]], HG_PASSES) for h in heads] for bi, (lo, mid, _, _) in enumerate(blocks)]
        zero = jnp.zeros((half, HG_DIM), F32)
        for h in heads:
            o[h] = o[h] + jnp.concatenate([x for bi in range(len(blocks)) for x in (zero, pv[bi][h])], axis=0)
        size = half
        yield

    b_last = b[L - 1:L, :]
    q_dec = q * jnp.exp(b)
    k_end = k * jnp.exp(b_last - b)
    w_end = jnp.exp(b_last)
    st = [st_ref[h] for h in heads]
    o_st = [_mm_nt(q_dec[:, hs[h]], st[h], HG_PASSES) for h in heads]
    upd = [_mm(v[:, hs[h]].T, k_end[:, hs[h]], HG_PASSES) for h in heads]
    for h in heads:
        st_ref[h] = st[h] * w_end[:, hs[h]] + upd[h]
    yield

    zg = g_ref[...]
    gate = zg * jax.nn.sigmoid(zg)
    ng = ng_ref[...]
    for h in heads:
        y = o[h] + o_st[h]
        y = y * lax.rsqrt(jnp.mean(y * y, axis=-1, keepdims=True) + EPS) * ng[:, hs[h]]
        o_ref[:, hs[h]] = (y * gate[:, hs[h]]).astype(o_ref.dtype)


def _rwkv_pre_kernel(zr_ref, zk_ref, zv_ref, zg_ref, zwa_ref,
                     mr_ref, mk_ref, mv_ref, mg_ref, mwa_ref,
                     w0_ref, w2_ref, a0_ref, a2_ref, g2_ref, kk_ref, ka_ref,
                     r_out, lw_out, k_out, v_out, kk_out, a_out, g_out,
                     er_ref, ek_ref, ev_ref, eg_ref, ewa_ref, *, tm, tp, pad, nb):
    i = pl.program_id(0)

    def shifted(z_ref, e_ref, mu_ref):
        @pl.when(i == 0)
        def _():
            e_ref[0:SUBLANES, :] = jnp.zeros((SUBLANES, e_ref.shape[1]), F32)

        x = z_ref[...]
        e_ref[SUBLANES:SUBLANES + tm, :] = x
        prev = e_ref[SUBLANES - 1:SUBLANES - 1 + tm, :]
        e_ref[0:SUBLANES, :] = x[tm - SUBLANES:tm, :]
        return x + mu_ref[...] * (prev - x)

    xr = shifted(zr_ref, er_ref, mr_ref)
    xk = shifted(zk_ref, ek_ref, mk_ref)
    xv = shifted(zv_ref, ev_ref, mv_ref)
    xg = shifted(zg_ref, eg_ref, mg_ref)
    xwa = shifted(zwa_ref, ewa_ref, mwa_ref)

    row = i * tm + lax.broadcasted_iota(jnp.int32, (tm, 1), 0)
    valid = _row_valid(row, tp, pad, nb)

    w = w0_ref[...] + jnp.dot(jnp.tanh(xwa), w2_ref[...], precision=HIGHEST, preferred_element_type=F32)
    w = -_softplus(-w) - 0.5
    lw_out[...] = -jnp.exp(w)
    a = jax.nn.sigmoid(a0_ref[...] + jnp.dot(xwa, a2_ref[...], precision=HIGHEST, preferred_element_type=F32))
    a_out[...] = a.astype(a_out.dtype)
    g_out[...] = jnp.dot(jax.nn.sigmoid(xg).astype(BF16), g2_ref[...],
                         preferred_element_type=F32).astype(g_out.dtype)
    r_out[...] = xr.astype(r_out.dtype)
    v_out[...] = jnp.where(valid, xv, 0.0).astype(v_out.dtype)
    k_out[...] = jnp.where(valid, xk * (1.0 + (a - 1.0) * ka_ref[...]), 0.0).astype(k_out.dtype)
    kk_out[...] = jnp.where(valid, xk * kk_ref[...], 0.0).astype(kk_out.dtype)


def _rwkv_pre(z, mu, w0, w2, a0, a2, g2, k_k, k_a, *, nb, tp, pad, tm):
    m = z.shape[0]
    mu_r, mu_k, mu_v = (mu[s:s + RW_WIDTH].reshape(1, RW_WIDTH) for s in (0, RW_WIDTH, 2 * RW_WIDTH))
    o = 3 * RW_WIDTH
    mu_wa = mu[o:o + 128].reshape(1, 128)
    mu_g = jnp.pad(mu[o + 128:o + 128 + RW_GATE_LORA], (0, 256 - RW_GATE_LORA)).reshape(1, 256)
    w2p = jnp.concatenate([w2, jnp.zeros_like(a2)], axis=0)
    a2p = jnp.concatenate([jnp.zeros_like(w2), a2], axis=0)
    g2p = jnp.pad(g2, ((0, 256 - RW_GATE_LORA), (0, 0)))
    kern = functools.partial(_rwkv_pre_kernel, tm=tm, tp=tp, pad=pad, nb=nb)

    def zspec(off, width):
        return pl.BlockSpec((tm, width), lambda i: (i, off // width))

    def full(shape):
        return pl.BlockSpec(shape, lambda i: (0,) * len(shape))

    vec = full((1, RW_WIDTH))
    wide = pl.BlockSpec((tm, RW_WIDTH), lambda i: (i, 0))
    outs = pl.pallas_call(
        kern,
        grid=(m // tm,),
        in_specs=[zspec(Z_R, RW_WIDTH), zspec(Z_K, RW_WIDTH), zspec(Z_V, RW_WIDTH), zspec(Z_GL, 256), zspec(Z_WA, 128),
                  vec, vec, vec, full((1, 256)), full((1, 128)),
                  vec, full((128, RW_WIDTH)), vec, full((128, RW_WIDTH)), full((256, RW_WIDTH)), vec, vec],
        out_specs=[wide] * 7,
        out_shape=[jax.ShapeDtypeStruct((m, RW_WIDTH), F32 if n == 1 else BF16) for n in range(7)],
        scratch_shapes=[pltpu.VMEM((tm + SUBLANES, RW_WIDTH), F32)] * 3
        + [pltpu.VMEM((tm + SUBLANES, 256), F32), pltpu.VMEM((tm + SUBLANES, 128), F32)],
        compiler_params=_params(("arbitrary",)),
        name="rwkv_pre",
    )(z, z, z, z, z, mu_r, mu_k, mu_v, mu_g, mu_wa,
      w0.reshape(1, -1), w2p, a0.reshape(1, -1), a2p, g2p.astype(BF16), k_k.reshape(1, -1), k_a.reshape(1, -1))
    return outs


_NN = (((1,), (0,)), ((), ()))
_NT = (((1,), (1,)), ((), ()))


def _split2(a):
    hi = a.astype(BF16)
    return hi, (a - hi.astype(F32)).astype(BF16)


def _dot(a, b, dims, passes):
    if passes == 6:
        return lax.dot_general(a, b, dims, precision=HIGHEST, preferred_element_type=F32)
    if passes == 1:
        return lax.dot_general(a.astype(BF16), b.astype(BF16), dims, preferred_element_type=F32)
    ah, al = _split2(a)
    bh, bl = _split2(b)
    out = lax.dot_general(ah, bh, dims, preferred_element_type=F32)
    out = out + lax.dot_general(ah, bl, dims, preferred_element_type=F32)
    return out + lax.dot_general(al, bh, dims, preferred_element_type=F32)


def _mm(a, b, passes=6):
    return _dot(a, b, _NN, passes)


def _mm_nt(a, b, passes=6):
    return _dot(a, b, _NT, passes)


def _bf(x):
    return x.astype(BF16)


def _bmm(a, b):
    return lax.dot_general(a, b, _NN, preferred_element_type=F32)


def _bmm_nt(a, b):
    return lax.dot_general(a, b, _NT, preferred_element_type=F32)


def _geometric_sums(xs, eye, steps):
    n = xs[0].shape[0]
    s = [eye + x for x in xs]
    if steps == 1:
        return s
    p = [_bmm(b, b) for b in map(_bf, xs)]
    yield
    for k in range(1, steps):
        pb = [_bf(pp) for pp in p]
        if k == steps - 1:
            s = [ss + _bmm(b, _bf(ss)) for ss, b in zip(s, pb)]
        else:
            both = [_bmm(b, jnp.concatenate([b, _bf(ss)], axis=1)) for b, ss in zip(pb, s)]
            p = [r[:, 0:n] for r in both]
            s = [ss + r[:, n:2 * n] for ss, r in zip(s, both)]
        yield
    return s


def _unit_lower_inverse(n_mats, blk):
    n = n_mats[0].shape[0]
    ri = lax.broadcasted_iota(jnp.int32, (n, n), 0)
    ci = lax.broadcasted_iota(jnp.int32, (n, n), 1)
    eye = (ri == ci).astype(F32)
    diag_blk = _block_id(ri, blk) == _block_id(ci, blk)
    d_inv = yield from _geometric_sums([jnp.where(diag_blk, -m, 0.0) for m in n_mats], eye, int(np.log2(blk)))
    d_inv_b = [_bf(d) for d in d_inv]
    m2 = [-_bmm(d, _bf(jnp.where(diag_blk, 0.0, m))) for d, m in zip(d_inv_b, n_mats)]
    yield
    t = yield from _geometric_sums(m2, eye, int(np.log2(RW_CHUNK // blk)))
    t_inv = [_bmm(_bf(tt), d) for tt, d in zip(t, d_inv_b)]
    yield
    return t_inv


def _rwkv_stages(r_ref, lw_ref, k_ref, v_ref, kk_ref, a_ref, g_ref, rk_ref, lng_ref, lnb_ref,
                 o_ref, st_ref, *, chunk, npair):
    @pl.when(pl.program_id(1) == 0)
    def _():
        st_ref[...] = jnp.zeros_like(st_ref)

    L = chunk
    n2 = 2 * L
    lane = lax.broadcasted_iota(jnp.int32, (1, LANES), 1)
    m0 = (lane < RW_HEAD).astype(F32)
    m1 = 1.0 - m0
    tri = (lax.broadcasted_iota(jnp.int32, (L, L), 0) >= lax.broadcasted_iota(jnp.int32, (L, L), 1)).astype(F32)
    ri = lax.broadcasted_iota(jnp.int32, (n2, n2), 0)
    ci = lax.broadcasted_iota(jnp.int32, (n2, n2), 1)
    strict = ri > ci
    incl = ri >= ci
    cum_all = _mm(tri, lw_ref[...])

    def stack(x):
        return jnp.concatenate([x * m0, x * m1], axis=0)

    def head_mean(x):
        s0 = jnp.sum(x * m0, axis=-1, keepdims=True)
        s1 = jnp.sum(x * m1, axis=-1, keepdims=True)
        return (s0 * m0 + s1 * m1) * (1.0 / RW_HEAD)

    pairs = range(npair)
    sls = [slice(p * LANES, (p + 1) * LANES) for p in pairs]
    ops = []
    for sl in sls:
        r = r_ref[:, sl].astype(F32)
        lw = lw_ref[:, sl]
        k = k_ref[:, sl].astype(F32)
        kk = kk_ref[:, sl].astype(F32)
        kn = kk / jnp.maximum(jnp.sqrt(head_mean(kk * kk) * RW_HEAD + 1e-24), 1e-12)
        cum = cum_all[:, sl]
        c_last = cum[L - 1:L, :]
        e_neg = jnp.exp(-cum)
        e_end = jnp.exp(c_last - cum)
        beta = kn * a_ref[:, sl].astype(F32)
        v_s = stack(v_ref[:, sl].astype(F32))
        ops.append(dict(
            al_t=_bf(stack(kn * jnp.exp(cum - lw))),
            r_t=_bf(stack(r * jnp.exp(cum))),
            be_h=_bf(stack(beta * e_neg)),
            k_h=_bf(stack(k * e_neg)),
            be_e=_bf(stack(beta * e_end)),
            k_e=_bf(stack(k * e_end)),
            v_s=v_s, v_b=_bf(v_s),
            w_end=jnp.exp(c_last)))
    yield

    sc = [_bmm_nt(jnp.concatenate([o["al_t"], o["r_t"]], axis=0), jnp.concatenate([o["be_h"], o["k_h"]], axis=0))
          for o in ops]
    a_ab = [jnp.where(strict, s[0:n2, 0:n2], 0.0) for s in sc]
    a_ak = [_bf(jnp.where(strict, s[0:n2, n2:2 * n2], 0.0)) for s in sc]
    a_rb = [_bf(jnp.where(incl, s[n2:2 * n2, 0:n2], 0.0)) for s in sc]
    a_rk = [_bf(jnp.where(incl, s[n2:2 * n2, n2:2 * n2], 0.0)) for s in sc]
    gy = [_bmm(jnp.concatenate([ak, rk], axis=0), o["v_b"]) for ak, rk, o in zip(a_ak, a_rk, ops)]
    yield
    t_inv = yield from _unit_lower_inverse(a_ab, 16)
    hats = [_bmm(_bf(t), jnp.concatenate([o["al_t"], _bf(g[0:n2])], axis=1)) for t, o, g in zip(t_inv, ops, gy)]
    yield

    st = [st_ref[p] for p in pairs]
    from_st = [_bmm_nt(jnp.concatenate([_bf(h[:, 0:LANES]), o["r_t"]], axis=0), _bf(s))
               for h, o, s in zip(hats, ops, st)]
    u = [-(f[0:n2] + h[:, LANES:2 * LANES]) for f, h in zip(from_st, hats)]
    yield
    y_u = [_bmm(a, _bf(uu)) for a, uu in zip(a_rb, u)]
    upd = [_bmm(_bf(jnp.concatenate([uu, o["v_s"]], axis=0).T), jnp.concatenate([o["be_e"], o["k_e"]], axis=0))
           for uu, o in zip(u, ops)]
    for p in pairs:
        st_ref[p] = st[p] * ops[p]["w_end"] + upd[p]
    yield
    y_st = [f[n2:2 * n2] for f in from_st]
    y_loc = [g[n2:2 * n2] for g in gy]

    for p, sl in enumerate(sls):
        y = y_st[p] + y_u[p] + y_loc[p]
        y = y[0:L] + y[L:n2]
        r = r_ref[:, sl].astype(F32)
        k = k_ref[:, sl].astype(F32)
        v = v_ref[:, sl].astype(F32)
        mu = head_mean(y)
        yc = y - mu
        var = head_mean(yc * yc)
        yn = yc * lax.rsqrt(var + RW_GN_EPS) * lng_ref[:, sl] + lnb_ref[:, sl]
        bonus = head_mean(r * k * rk_ref[:, sl]) * RW_HEAD
        o_ref[:, sl] = ((yn + bonus * v) * g_ref[:, sl]).astype(o_ref.dtype)


def _recurrences_kernel(hq_ref, hf_ref, hi_ref, hg_ref, lb_ref, ng_ref,
                        r_ref, lw_ref, k_ref, v_ref, kk_ref, a_ref, g_ref, rk_ref, lng_ref, lnb_ref,
                        ya_ref, yb_ref, hst_ref, rst_ref, *, pad):
    stages = [_rwkv_stages(r_ref, lw_ref, k_ref, v_ref, kk_ref, a_ref, g_ref, rk_ref, lng_ref, lnb_ref,
                           yb_ref, rst_ref, chunk=RW_CHUNK, npair=RW_WIDTH // LANES),
              _hgrn_stages(hq_ref, hf_ref, hi_ref, hg_ref, lb_ref, ng_ref, ya_ref, hst_ref,
                           chunk=HG_CHUNK, pad=pad, nheads=HG_HEADS)]
    while stages:
        for gen in list(stages):
            if next(gen, stages) is stages:
                stages.remove(gen)


def _recurrences(z, lb, norm_g, pre, r_k, ln_g, ln_b, *, nb, tp, pad):
    assert HG_CHUNK == RW_CHUNK and HG_WIDTH == RW_WIDTH
    m = z.shape[0]
    L = HG_CHUNK
    nc = tp // L

    def zspec(off):
        return pl.BlockSpec((L, HG_WIDTH), lambda b, c: (b * nc + c, off // HG_WIDTH))

    blk = pl.BlockSpec((L, RW_WIDTH), lambda b, c: (b * nc + c, 0))
    pspec = pl.BlockSpec((1, RW_WIDTH), lambda b, c: (0, 0))
    row = lambda x: x.reshape(1, -1).astype(F32)
    return pl.pallas_call(
        functools.partial(_recurrences_kernel, pad=pad),
        grid=(nb, nc),
        in_specs=[zspec(Z_HQ), zspec(Z_HF), zspec(Z_HI), zspec(Z_HG), pspec, pspec] + [blk] * 7 + [pspec] * 3,
        out_specs=[blk, blk],
        out_shape=[jax.ShapeDtypeStruct((m, HG_WIDTH), BF16), jax.ShapeDtypeStruct((m, RW_WIDTH), BF16)],
        scratch_shapes=[pltpu.VMEM((HG_HEADS, HG_DIM, HG_DIM), F32),
                        pltpu.VMEM((RW_WIDTH // LANES, LANES, LANES), F32)],
        compiler_params=_params(("arbitrary", "arbitrary")),
        name="recurrences",
    )(z, z, z, z, row(lb), row(norm_g), *pre, row(r_k), row(ln_g), row(ln_b))


def _rope_kernel(x_ref, cos_ref, sin_ref, o_ref, *, n_plain, n_rope, scale, is_query, tr, nrb, pad):
    lane = lax.broadcasted_iota(jnp.int32, (1, LANES), 1)
    cos = cos_ref[...]
    sin = sin_ref[...]
    if is_query:
        bias = jnp.ones((tr, 1), F32)
    else:
        pos = (pl.program_id(0) % nrb) * tr + lax.broadcasted_iota(jnp.int32, (tr, 1), 0)
        bias = jnp.where(pos < pad, NEG_BIG, 0.0)
    if n_plain:
        o_ref[:, 0:n_plain * LANES] = (x_ref[:, 0:n_plain * LANES] * scale).astype(o_ref.dtype)
    for h in range(n_plain, n_plain + n_rope):
        x = x_ref[:, h * LANES:(h + 1) * LANES]
        rot = jnp.where(lane < MLA_ROPE // 2, -pltpu.roll(x, LANES - MLA_ROPE // 2, 1),
                        pltpu.roll(x, MLA_ROPE // 2, 1))
        y = (x * cos + rot * sin) * scale
        o_ref[:, h * LANES:(h + 1) * LANES] = jnp.where(lane == MLA_ROPE, bias, y).astype(o_ref.dtype)


def _rope(x, col_block, n_plain, n_rope, cos, sin, scale, is_query, *, tp, tr, pad):
    m = x.shape[0]
    nrb = tp // tr
    width = (n_plain + n_rope) * LANES
    kern = functools.partial(_rope_kernel, n_plain=n_plain, n_rope=n_rope, scale=scale, is_query=is_query,
                             tr=tr, nrb=nrb, pad=pad)
    tab = pl.BlockSpec((tr, LANES), lambda i: (i % nrb, 0))
    return pl.pallas_call(
        kern,
        grid=(m // tr,),
        in_specs=[pl.BlockSpec((tr, width), lambda i: (i, col_block)), tab, tab],
        out_specs=pl.BlockSpec((tr, width), lambda i: (i, 0)),
        out_shape=jax.ShapeDtypeStruct((m, width), BF16),
        compiler_params=_params(("arbitrary",)),
        name="rope",
    )(x, cos, sin)


def _q_proj_kernel(x_ref, g_ref, w_ref, cos_ref, sin_ref, o_ref, xn_ref, *, n_plain, scale, eps):
    j = pl.program_id(1)

    @pl.when(j == 0)
    def _():
        xn_ref[...] = _normed_rows(x_ref[...], g_ref[...], eps, 0, 0, 0, 0).astype(BF16)

    y = jnp.dot(xn_ref[...], w_ref[...], preferred_element_type=F32)

    @pl.when(j < n_plain)
    def _():
        o_ref[...] = (y * scale).astype(o_ref.dtype)

    @pl.when(j >= n_plain)
    def _():
        lane = lax.broadcasted_iota(jnp.int32, (1, LANES), 1)
        cos = cos_ref[...]
        sin = sin_ref[...]
        for b in range(y.shape[1] // LANES):
            x = y[:, b * LANES:(b + 1) * LANES]
            rot = jnp.where(lane < MLA_ROPE // 2, -pltpu.roll(x, LANES - MLA_ROPE // 2, 1),
                            pltpu.roll(x, MLA_ROPE // 2, 1))
            r = (x * cos + rot * sin) * scale
            o_ref[:, b * LANES:(b + 1) * LANES] = jnp.where(lane == MLA_ROPE, 1.0, r).astype(o_ref.dtype)


def _q_proj(z, g, w, cos_rows, sin_rows, scale, *, tm, tn):
    m = z.shape[0]
    k = MLA_Q_RANK
    n = w.shape[1]
    kern = functools.partial(_q_proj_kernel, n_plain=MLA_HEADS * MLA_NOPE // tn, scale=scale, eps=EPS)
    tab = pl.BlockSpec((tm, LANES), lambda i, j: (i, 0))
    return pl.pallas_call(
        kern,
        grid=(m // tm, n // tn),
        in_specs=[pl.BlockSpec((tm, k), lambda i, j: (i, Z_CQ // k)),
                  pl.BlockSpec((1, k), lambda i, j: (0, 0)),
                  pl.BlockSpec((k, tn), lambda i, j: (0, j)), tab, tab],
        out_specs=pl.BlockSpec((tm, tn), lambda i, j: (i, j)),
        out_shape=jax.ShapeDtypeStruct((m, n), BF16),
        scratch_shapes=[pltpu.VMEM((tm, k), BF16)],
        compiler_params=_params(("arbitrary", "arbitrary")),
        name="q_proj",
    )(z, g.reshape(1, k).astype(F32), w, cos_rows, sin_rows)


def _attn_kernel(qn_ref, qr_ref, qn_next_ref, qr_next_ref, kn_ref, kr_ref, vt_ref, bias_ref, o_ref,
                 sg_ref, s0_ref, s1_ref, m_ref, l_ref, acc_ref, *, t, nq):
    qi = pl.program_id(2)
    q = jnp.concatenate([qn_ref[...], qr_ref[...]], axis=1)

    def scores_t(off):
        kcat = jnp.concatenate([kn_ref[pl.ds(off, 4 * t), :], kr_ref[pl.ds(off, 4 * t), :]], axis=1)
        return lax.dot_general(kcat, q, _NT, preferred_element_type=F32)

    def first_tile(jq):
        return jnp.minimum(4 * jq, nq - 4)

    def issue_scores(jq, dst_ref):
        first = first_tile(jq)
        s = scores_t(pl.multiple_of(first * t, t))
        for n in range(4):
            x = first + n
            kind = jnp.where(x < 4 * jq, 2, jnp.where(x < qi, 0, jnp.where(x == qi, 1, 2)))
            dst_ref[n * t:(n + 1) * t, :] = s[n * t:(n + 1) * t] + bias_ref[kind]

    def consume(jq, src_ref):
        first = first_tile(jq)
        for stream in range(2):
            s = src_ref[2 * stream * t:2 * (stream + 1) * t, :]
            m_i = m_ref[stream]
            m_new = jnp.maximum(m_i, jnp.max(s, axis=0, keepdims=True))
            alpha = jnp.exp2(m_i - m_new)
            p = jnp.exp2(s - m_new)
            m_ref[stream] = m_new
            l_ref[stream] = alpha * l_ref[stream] + jnp.sum(p, axis=0, keepdims=True)
            v_t = jnp.concatenate([vt_ref[0, first + 2 * stream], vt_ref[0, first + 2 * stream + 1]], axis=1)
            acc_ref[stream] = alpha * acc_ref[stream] + jnp.dot(v_t, p.astype(BF16), preferred_element_type=F32)

    def half(jq, cur_ref, nxt_ref):
        issue_scores(jq + 1, nxt_ref)
        consume(jq, cur_ref)

    def issue_next_tile():
        q_next = jnp.concatenate([qn_next_ref[...], qr_next_ref[...]], axis=1)
        kcat = jnp.concatenate([kn_ref[0:4 * t, :], kr_ref[0:4 * t, :]], axis=1)
        sg_ref[...] = lax.dot_general(kcat, q_next, _NT, preferred_element_type=F32)

    last = qi // 4
    m_ref[...] = jnp.full(m_ref.shape, NEG_BIG, F32)
    l_ref[...] = jnp.zeros(l_ref.shape, F32)
    acc_ref[...] = jnp.zeros(acc_ref.shape, F32)

    @pl.when(qi <= 4)
    def _():
        issue_scores(0, sg_ref)

    @pl.when(last == 0)
    def _():
        consume(0, sg_ref)

    @pl.when(last > 0)
    def _():
        issue_scores(1, s0_ref)
        consume(0, sg_ref)

        def trip(r, carry):
            half(1 + 2 * r, s0_ref, s1_ref)

            @pl.when(2 + 2 * r < last)
            def _():
                half(2 + 2 * r, s1_ref, s0_ref)

            return carry

        lax.fori_loop(0, last // 2, trip, 0)

        @pl.when(last % 2 == 1)
        def _():
            issue_next_tile()
            consume(last, s0_ref)

        @pl.when(last % 2 == 0)
        def _():
            issue_next_tile()
            consume(last, s1_ref)

    m_a, m_b = m_ref[0], m_ref[1]
    m_i = jnp.maximum(m_a, m_b)
    w_a = jnp.exp2(m_a - m_i)
    w_b = jnp.exp2(m_b - m_i)
    out_t = (w_a * acc_ref[0] + w_b * acc_ref[1]) / (w_a * l_ref[0] + w_b * l_ref[1])
    o_ref[...] = out_t.T.astype(o_ref.dtype)


def _attention(q, kr, kv, *, nb, tp):
    m = q.shape[0]
    tq = _pick(tp, tuple(c for c in (384, 256, 128) if tp >= 4 * c))
    nq = tp // tq
    kern = functools.partial(_attn_kernel, t=tq, nq=nq)
    allowed_t = jnp.arange(tq)[:, None] <= jnp.arange(tq)[None, :]
    bias = jnp.stack([jnp.zeros((tq, tq), F32), jnp.where(allowed_t, 0.0, NEG_BIG).astype(F32),
                      jnp.full((tq, tq), NEG_BIG, F32)])
    v_t = kv[:, MLA_HEADS * MLA_NOPE:].reshape(nb, nq, tq, MLA_HEADS, MLA_V)
    v_t = v_t.transpose(0, 3, 1, 4, 2).reshape(nb * MLA_HEADS, nq, MLA_V, tq)
    return pl.pallas_call(
        kern,
        grid=(nb, MLA_HEADS, nq),
        in_specs=[pl.BlockSpec((tq, LANES), lambda b, h, i: (b * nq + i, h)),
                  pl.BlockSpec((tq, LANES), lambda b, h, i: (b * nq + i, MLA_HEADS + h)),
                  pl.BlockSpec((tq, LANES), lambda b, h, i: (b * nq + jnp.minimum(i + 1, nq - 1), h)),
                  pl.BlockSpec((tq, LANES), lambda b, h, i: (b * nq + jnp.minimum(i + 1, nq - 1), MLA_HEADS + h)),
                  pl.BlockSpec((tp, LANES), lambda b, h, i: (b, h)),
                  pl.BlockSpec((tp, LANES), lambda b, h, i: (b, 0)),
                  pl.BlockSpec((1, nq, MLA_V, tq), lambda b, h, i: (b * MLA_HEADS + h, 0, 0, 0)),
                  pl.BlockSpec((3, tq, tq), lambda b, h, i: (0, 0, 0))],
        out_specs=pl.BlockSpec((tq, LANES), lambda b, h, i: (b * nq + i, h)),
        out_shape=jax.ShapeDtypeStruct((m, MLA_WIDTH), BF16),
        scratch_shapes=[pltpu.VMEM((4 * tq, tq), F32)] * 3
        + [pltpu.VMEM((2, 1, tq), F32), pltpu.VMEM((2, 1, tq), F32), pltpu.VMEM((2, MLA_V, tq), F32)],
        compiler_params=_params(("arbitrary", "arbitrary", "arbitrary")),
        name="mla_attention",
    )(q, q, q, q, kv, kr, v_t, bias)


def _merge_kernel(ya_ref, yb_ref, yc_ref, ga_ref, gb_ref, gc_ref, w_ref, o_ref):
    acc = None
    for n, (y_ref, g_ref) in enumerate(((ya_ref, ga_ref), (yb_ref, gb_ref), (yc_ref, gc_ref))):
        p = jnp.dot(y_ref[...], w_ref[n], preferred_element_type=F32)
        t = jax.nn.sigmoid(g_ref[...]) * p
        acc = t if acc is None else acc + t
    o_ref[...] = acc.astype(o_ref.dtype)


def _merge(ya, yb, yc, z, w_branch, *, tm, tn):
    m = ya.shape[0]
    d = w_branch.shape[2]
    yspec = pl.BlockSpec((tm, HG_WIDTH), lambda i, j: (i, 0))

    def gspec(n):
        return pl.BlockSpec((tm, tn), lambda i, j: (i, (Z_GATES + n * d) // tn + j))

    return pl.pallas_call(
        _merge_kernel,
        grid=(m // tm, d // tn),
        in_specs=[yspec, yspec, yspec, gspec(0), gspec(1), gspec(2),
                  pl.BlockSpec((N_BRANCH, HG_WIDTH, tn), lambda i, j: (0, 0, j))],
        out_specs=pl.BlockSpec((tm, tn), lambda i, j: (i, j)),
        out_shape=jax.ShapeDtypeStruct((m, d), BF16),
        compiler_params=_params(("arbitrary", "arbitrary")),
        name="merge",
    )(ya, yb, yc, z, z, z, w_branch)


def _matmul_res_kernel(x_ref, w_ref, r_ref, o_ref):
    o_ref[...] = r_ref[...] + jnp.dot(x_ref[...], w_ref[...], preferred_element_type=F32)


def _matmul_res(x, w, res, *, tm, tn):
    m, k = x.shape
    n = w.shape[1]
    return pl.pallas_call(
        _matmul_res_kernel,
        grid=(m // tm, n // tn),
        in_specs=[pl.BlockSpec((tm, k), lambda i, j: (i, 0)),
                  pl.BlockSpec((k, tn), lambda i, j: (0, j)),
                  pl.BlockSpec((tm, tn), lambda i, j: (i, j))],
        out_specs=pl.BlockSpec((tm, tn), lambda i, j: (i, j)),
        out_shape=jax.ShapeDtypeStruct((m, n), F32),
        compiler_params=_params(("arbitrary", "arbitrary")),
        name="matmul_res",
    )(x, w, res)


def _ffn_up_kernel(x_ref, g_ref, wg_ref, wv_ref, cwg_ref, cwv_ref, cbg_ref, cbv_ref, o_ref,
                   xn_ref, eg_ref, ev_ref, pg_ref, pv_ref, *, tm, tp, pad, nb, eps):
    i = pl.program_id(0)
    j = pl.program_id(1)

    @pl.when(j == 0)
    def _():
        xn_ref[...] = _normed_rows(x_ref[...], g_ref[...], eps, i * tm, tp, pad, nb).astype(BF16)

    @pl.when(i == 0)
    def _():
        pg_ref[j] = jnp.zeros(pg_ref.shape[1:], F32)
        pv_ref[j] = jnp.zeros(pv_ref.shape[1:], F32)

    def conv(w_ref, e_ref, p_ref, cw_ref, cb_ref):
        u = jnp.dot(xn_ref[...], w_ref[...], preferred_element_type=F32)
        e_ref[0:SUBLANES, :] = p_ref[j]
        e_ref[SUBLANES:SUBLANES + tm, :] = u
        p_ref[j] = u[tm - SUBLANES:tm, :]
        u1 = e_ref[SUBLANES - 1:SUBLANES - 1 + tm, :]
        u2 = e_ref[SUBLANES - 2:SUBLANES - 2 + tm, :]
        cw = cw_ref[...]
        return cw[0:1, :] * u2 + cw[1:2, :] * u1 + cw[2:3, :] * u + cb_ref[...]

    gate = conv(wg_ref, eg_ref, pg_ref, cwg_ref, cbg_ref)
    val = conv(wv_ref, ev_ref, pv_ref, cwv_ref, cbv_ref)
    o_ref[...] = (gate * jax.nn.sigmoid(gate) * val).astype(o_ref.dtype)


def _ffn_up(h, g, w_up, conv_w, conv_b, *, tp, pad, nb, tm, tf):
    m, d = h.shape
    dff = w_up.shape[1] // 2
    nj = dff // tf
    kern = functools.partial(_ffn_up_kernel, tm=tm, tp=tp, pad=pad, nb=nb, eps=EPS)
    cb = conv_b.reshape(1, -1).astype(F32)
    return pl.pallas_call(
        kern,
        grid=(m // tm, nj),
        in_specs=[pl.BlockSpec((tm, d), lambda i, j: (i, 0)),
                  pl.BlockSpec((1, d), lambda i, j: (0, 0)),
                  pl.BlockSpec((d, tf), lambda i, j: (0, j)),
                  pl.BlockSpec((d, tf), lambda i, j: (0, nj + j)),
                  pl.BlockSpec((CONV_W, tf), lambda i, j: (0, j)),
                  pl.BlockSpec((CONV_W, tf), lambda i, j: (0, nj + j)),
                  pl.BlockSpec((1, tf), lambda i, j: (0, j)),
                  pl.BlockSpec((1, tf), lambda i, j: (0, nj + j))],
        out_specs=pl.BlockSpec((tm, tf), lambda i, j: (i, j)),
        out_shape=jax.ShapeDtypeStruct((m, dff), BF16),
        scratch_shapes=[pltpu.VMEM((tm, d), BF16),
                        pltpu.VMEM((tm + SUBLANES, tf), F32), pltpu.VMEM((tm + SUBLANES, tf), F32),
                        pltpu.VMEM((nj, SUBLANES, tf), F32), pltpu.VMEM((nj, SUBLANES, tf), F32)],
        compiler_params=_params(("arbitrary", "arbitrary")),
        name="ffn_up",
    )(h, g.reshape(1, d).astype(F32), w_up, w_up, conv_w.astype(F32), conv_w.astype(F32), cb, cb)


def _final_norm_kernel(x_ref, g_ref, o_ref):
    x = x_ref[0]
    o_ref[0] = x * lax.rsqrt(jnp.mean(x * x, axis=-1, keepdims=True) + EPS) * g_ref[...]


def _final_norm(h3, g, *, skip, seq):
    nb, tp, d = h3.shape
    tr = LANES
    return pl.pallas_call(
        _final_norm_kernel,
        grid=(nb, seq // tr),
        in_specs=[pl.BlockSpec((1, tr, d), lambda b, i: (b, skip // tr + i, 0)),
                  pl.BlockSpec((1, d), lambda b, i: (0, 0))],
        out_specs=pl.BlockSpec((1, tr, d), lambda b, i: (b, i, 0)),
        out_shape=jax.ShapeDtypeStruct((nb, seq, d), F32),
        compiler_params=_params(("arbitrary", "arbitrary")),
        name="final_norm",
    )(h3, g.reshape(1, d).astype(F32))


def _layout_w_in(w):
    d = w.shape[0]
    o = 0
    parts = {}
    for name, width in (("hg4", 4 * HG_WIDTH), ("rkv", 3 * RW_WIDTH), ("wl", RW_DECAY_LORA), ("al", RW_AAA_LORA),
                        ("gl", RW_GATE_LORA), ("cq", MLA_Q_RANK), ("ckv", MLA_KV_RANK), ("kr", MLA_ROPE),
                        ("gates", N_BRANCH * d)):
        parts[name] = w[:, o:o + width]
        o += width
    zeros = lambda n: jnp.zeros((d, n), w.dtype)
    cols = [parts["hg4"], parts["rkv"], parts["cq"], parts["ckv"],
            parts["gl"], zeros(256 - RW_GATE_LORA), parts["wl"], parts["al"],
            parts["kr"], zeros(LANES - MLA_ROPE), parts["gates"]]
    return jnp.concatenate(cols, axis=1).astype(BF16)


def _layout_w_uq(w):
    r = w.shape[0]
    w = w.reshape(r, MLA_HEADS, MLA_NOPE + MLA_ROPE)
    nope = w[:, :, :MLA_NOPE].reshape(r, MLA_HEADS * MLA_NOPE)
    rope = jnp.pad(w[:, :, MLA_NOPE:], ((0, 0), (0, 0), (0, LANES - MLA_ROPE))).reshape(r, MLA_HEADS * LANES)
    return jnp.concatenate([nope, rope], axis=1).astype(BF16)


def _layout_w_ukv(w):
    r = w.shape[0]
    w = w.reshape(r, MLA_HEADS, MLA_NOPE + MLA_V)
    return jnp.concatenate([w[:, :, :MLA_NOPE].reshape(r, -1), w[:, :, MLA_NOPE:].reshape(r, -1)], axis=1).astype(BF16)


def _rope_tables(tp, pad):
    inv = ROPE_BASE ** (-jnp.arange(0, MLA_ROPE, 2, dtype=F32) / MLA_ROPE)
    pos = jnp.arange(tp, dtype=F32) - float(pad)
    ang = pos[:, None] * inv[None, :]
    z = jnp.zeros((tp, LANES - MLA_ROPE), F32)
    cos = jnp.concatenate([jnp.cos(ang), jnp.cos(ang), z], axis=1)
    sin = jnp.concatenate([jnp.sin(ang), jnp.sin(ang), z], axis=1)
    return cos, sin


def kernel(x, meta_tokens, norm1_g, w_in, hg_lb_logits, hg_norm_g, rw_mu, rw_w0, rw_w2, rw_a0, rw_a2, rw_g2, rw_k_k, rw_k_a, rw_r_k, rw_ln_g, rw_ln_b, mla_q_norm_g, mla_w_uq, mla_kv_norm_g, mla_w_ukv, w_branch, w_out, norm2_g, ffn_w_up, ffn_conv_w, ffn_conv_b, ffn_w_down, final_norm_g):
    nb, seq, d = x.shape
    depth = w_in.shape[0]
    t = N_META + seq
    pad = (-N_META) % LANES
    tp = pad + t
    m = nb * tp
    tm = _pick(m, (768, 384, 256, 128))
    tm_wide = _pick(m, (1376, 768, 384, 256, 128))
    geo = dict(tp=tp, pad=pad, nb=nb)

    meta = jnp.broadcast_to(meta_tokens[None].astype(x.dtype), (nb, N_META, d))
    h = jnp.concatenate([jnp.zeros((nb, pad, d), x.dtype), meta, x], axis=1).reshape(m, d)
    cos, sin = _rope_tables(tp, pad)
    cos_rows, sin_rows = jnp.tile(cos, (nb, 1)), jnp.tile(sin, (nb, 1))
    p_lb = jax.nn.softmax(hg_lb_logits.astype(F32), axis=0)
    lower_bounds = jnp.cumsum(p_lb, axis=0) - p_lb[0]
    scale = float((MLA_NOPE + MLA_ROPE) ** -0.5 * np.log2(np.e))

    for l in range(depth):
        z = _norm_matmul(h, 0, d, norm1_g[l], _layout_w_in(w_in[l]), F32, tm=tm_wide, tn=512, **geo)
        pre = _rwkv_pre(z, rw_mu[l], rw_w0[l], rw_w2[l], rw_a0[l], rw_a2[l], rw_g2[l], rw_k_k[l], rw_k_a[l],
                        tm=_pick(m, (384, 256, 128)), **geo)
        y_a, y_b = _recurrences(z, lower_bounds[l], hg_norm_g[l], pre, rw_r_k[l], rw_ln_g[l], rw_ln_b[l], **geo)
        q_cat = _q_proj(z, mla_q_norm_g[l], _layout_w_uq(mla_w_uq[l]), cos_rows, sin_rows, scale, tm=tm_wide, tn=512)
        kv = _norm_matmul(z, Z_CKV // MLA_KV_RANK, MLA_KV_RANK, mla_kv_norm_g[l], _layout_w_ukv(mla_w_ukv[l]), BF16,
                          tm=tm_wide, tn=512, tp=tp, pad=0, nb=nb)
        tr = _pick(tp, (384, 256, 128))
        k_rope = _rope(z, Z_KR // LANES, 0, 1, cos, sin, 1.0, False, tp=tp, tr=tr, pad=pad)
        y_c = _attention(q_cat, k_rope, kv, nb=nb, tp=tp)
        merged = _merge(y_a, y_b, y_c, z, w_branch[l].astype(BF16), tm=tm, tn=512)
        h = _matmul_res(merged, w_out[l].astype(BF16), h, tm=tm_wide, tn=512)
        act = _ffn_up(h, norm2_g[l], ffn_w_up[l].astype(BF16), ffn_conv_w[l], ffn_conv_b[l], tm=tm, tf=512, **geo)
        h = _matmul_res(act, ffn_w_down[l].astype(BF16), h, tm=tm, tn=512)
    return _final_norm(h.reshape(nb, tp, d), final_norm_g, skip=pad + N_META, seq=seq)
```

```python
import functools

import numpy as np
import jax
import jax.numpy as jnp
from jax import lax
from jax.experimental import pallas as pl
from jax.experimental.pallas import tpu as pltpu

F32 = jnp.float32
BF16 = jnp.bfloat16
HIGHEST = lax.Precision.HIGHEST

N_META = 16
EPS = 1e-6
NEG_BIG = -1e30
F_FLOOR = 1e-30
HG_HEADS = 8
HG_DIM = 128
HG_WIDTH = HG_HEADS * HG_DIM
RW_HEAD = 64
RW_HEADS = 16
RW_WIDTH = RW_HEADS * RW_HEAD
RW_DECAY_LORA = 64
RW_AAA_LORA = 64
RW_GATE_LORA = 160
RW_GN_EPS = 64e-5
MLA_HEADS = 8
MLA_Q_RANK = 512
MLA_KV_RANK = 512
MLA_NOPE = 128
MLA_ROPE = 64
MLA_V = 128
MLA_WIDTH = MLA_HEADS * MLA_V
ROPE_BASE = 10000.0
CONV_W = 3
N_BRANCH = 3

LANES = 128
SUBLANES = 8
VMEM_LIMIT = 56 * 1024 * 1024

Z_HQ, Z_HF, Z_HI, Z_HG = 0, 1024, 2048, 3072
Z_R, Z_K, Z_V = 4096, 5120, 6144
Z_CQ, Z_CKV = 7168, 7680
Z_GL = 8192
Z_WA = 8448
Z_KR = 8576
Z_GATES = 8704
Z_COLS = Z_GATES + N_BRANCH * 2048

RW_CHUNK = 64
HG_CHUNK = 64
HG_BASE = 4
HG_PASSES = 1


def _params(sem):
    return pltpu.CompilerParams(dimension_semantics=sem, vmem_limit_bytes=VMEM_LIMIT)


def _pick(n, cands):
    for c in cands:
        if n % c == 0:
            return c
    raise ValueError(f"no tile for {n} in {cands}")


def _block_id(idx, size):
    return lax.shift_right_logical(idx, int(np.log2(size)))


def _softplus(x):
    return jnp.maximum(x, 0.0) + jnp.log(1.0 + jnp.exp(-jnp.abs(x)))


def _row_valid(row, tp, pad, nb):
    valid = None
    for b in range(nb):
        ok = jnp.logical_or(row < b * tp, row >= b * tp + pad)
        valid = ok if valid is None else jnp.logical_and(valid, ok)
    return valid


def _normed_rows(x, g, eps, row0, tp, pad, nb):
    ms = jnp.mean(x * x, axis=-1, keepdims=True)
    y = x * lax.rsqrt(ms + eps) * g
    if pad:
        row = row0 + lax.broadcasted_iota(jnp.int32, (x.shape[0], 1), 0)
        y = jnp.where(_row_valid(row, tp, pad, nb), y, 0.0)
    return y


def _norm_matmul_kernel(x_ref, g_ref, w_ref, o_ref, xn_ref, *, tm, tp, pad, nb, eps):
    i = pl.program_id(0)

    @pl.when(pl.program_id(1) == 0)
    def _():
        xn_ref[...] = _normed_rows(x_ref[...], g_ref[...], eps, i * tm, tp, pad, nb).astype(BF16)

    o_ref[...] = jnp.dot(xn_ref[...], w_ref[...], preferred_element_type=F32).astype(o_ref.dtype)


def _norm_matmul(x, col_block, k, g, w, out_dtype, *, tp, pad, nb, tm, tn):
    m = x.shape[0]
    n = w.shape[1]
    kern = functools.partial(_norm_matmul_kernel, tm=tm, tp=tp, pad=pad, nb=nb, eps=EPS)
    return pl.pallas_call(
        kern,
        grid=(m // tm, n // tn),
        in_specs=[pl.BlockSpec((tm, k), lambda i, j: (i, col_block)),
                  pl.BlockSpec((1, k), lambda i, j: (0, 0)),
                  pl.BlockSpec((k, tn), lambda i, j: (0, j))],
        out_specs=pl.BlockSpec((tm, tn), lambda i, j: (i, j)),
        out_shape=jax.ShapeDtypeStruct((m, n), out_dtype),
        scratch_shapes=[pltpu.VMEM((tm, k), BF16)],
        compiler_params=_params(("arbitrary", "arbitrary")),
        name="norm_matmul",
    )(x, g.reshape(1, k).astype(F32), w)


def _hgrn_stages(q_ref, f_ref, i_ref, g_ref, lb_ref, ng_ref, o_ref, st_ref, *, chunk, pad, nheads):
    c = pl.program_id(1)

    @pl.when(c == 0)
    def _():
        st_ref[...] = jnp.zeros_like(st_ref)

    L = chunk
    base = HG_BASE
    heads = range(nheads)
    hs = [slice(h * HG_DIM, (h + 1) * HG_DIM) for h in heads]
    q = q_ref[...]
    zf = f_ref[...]
    v = i_ref[...]
    lb = lb_ref[...]
    row = lax.broadcasted_iota(jnp.int32, (L, 1), 0)
    valid = c * L + row >= pad
    f = lb + (1.0 - lb) * jax.nn.sigmoid(zf)
    logf = jnp.where(valid, jnp.log(jnp.maximum(f, F_FLOOR)), 0.0)
    k = jnp.where(valid, (1.0 - lb) * jax.nn.sigmoid(-zf), 0.0)
    tri = (row >= lax.broadcasted_iota(jnp.int32, (L, L), 1)).astype(F32)
    b = _mm(tri, logf)

    r = row & (base - 1)
    o = [jnp.zeros((L, HG_DIM), F32) for _ in heads]
    for lag in range(base):
        ks, bs, vs = (k, b, v) if lag == 0 else tuple(pltpu.roll(x, lag, 0) for x in (k, b, v))
        ok = r >= lag
        term = q * ks * jnp.exp(jnp.where(ok, b - bs, 0.0))
        for h in heads:
            a = jnp.sum(term[:, hs[h]], axis=-1, keepdims=True)
            o[h] = o[h] + jnp.where(ok, a, 0.0) * vs[:, hs[h]]
        yield

    size = L
    while size > base:
        half = size // 2
        blocks = []
        for lo in range(0, L, size):
            mid, hi = lo + half, lo + size
            b_mid = b[mid - 1:mid, :]
            blocks.append((lo, mid, q[mid:hi] * jnp.exp(b[mid:hi] - b_mid), k[lo:mid] * jnp.exp(b_mid - b[lo:mid])))
        sc = [[_mm_nt(qg[:, hs[h]], kg[:, hs[h]], HG_PASSES) for h in heads] for (_, _, qg, kg) in blocks]
        pv = [[_mm(sc[bi][h], v[lo:mid, hs[h]], HG_PASSES) for h in heads] for bi, (lo, mid, _, _) in enumerate(blocks)]
        zero = jnp.zeros((half, HG_DIM), F32)
        for h in heads:
            o[h] = o[h] + jnp.concatenate([x for bi in range(len(blocks)) for x in (zero, pv[bi][h])], axis=0)
        size = half
        yield

    b_last = b[L - 1:L, :]
    q_dec = q * jnp.exp(b)
    k_end = k * jnp.exp(b_last - b)
    w_end = jnp.exp(b_last)
    st = [st_ref[h] for h in heads]
    o_st = [_mm_nt(q_dec[:, hs[h]], st[h], HG_PASSES) for h in heads]
    upd = [_mm(v[:, hs[h]].T, k_end[:, hs[h]], HG_PASSES) for h in heads]
    for h in heads:
        st_ref[h] = st[h] * w_end[:, hs[h]] + upd[h]
    yield

    zg = g_ref[...]
    gate = zg * jax.nn.sigmoid(zg)
    ng = ng_ref[...]
    for h in heads:
        y = o[h] + o_st[h]
        y = y * lax.rsqrt(jnp.mean(y * y, axis=-1, keepdims=True) + EPS) * ng[:, hs[h]]
        o_ref[:, hs[h]] = (y * gate[:, hs[h]]).astype(o_ref.dtype)


def _rwkv_pre_kernel(zr_ref, zk_ref, zv_ref, zg_ref, zwa_ref,
                     mr_ref, mk_ref, mv_ref, mg_ref, mwa_ref,
                     w0_ref, w2_ref, a0_ref, a2_ref, g2_ref, kk_ref, ka_ref,
                     r_out, lw_out, k_out, v_out, kk_out, a_out, g_out,
                     er_ref, ek_ref, ev_ref, eg_ref, ewa_ref, *, tm, tp, pad, nb):
    i = pl.program_id(0)

    def shifted(z_ref, e_ref, mu_ref):
        @pl.when(i == 0)
        def _():
            e_ref[0:SUBLANES, :] = jnp.zeros((SUBLANES, e_ref.shape[1]), F32)

        x = z_ref[...]
        e_ref[SUBLANES:SUBLANES + tm, :] = x
        prev = e_ref[SUBLANES - 1:SUBLANES - 1 + tm, :]
        e_ref[0:SUBLANES, :] = x[tm - SUBLANES:tm, :]
        return x + mu_ref[...] * (prev - x)

    xr = shifted(zr_ref, er_ref, mr_ref)
    xk = shifted(zk_ref, ek_ref, mk_ref)
    xv = shifted(zv_ref, ev_ref, mv_ref)
    xg = shifted(zg_ref, eg_ref, mg_ref)
    xwa = shifted(zwa_ref, ewa_ref, mwa_ref)

    row = i * tm + lax.broadcasted_iota(jnp.int32, (tm, 1), 0)
    valid = _row_valid(row, tp, pad, nb)

    w = w0_ref[...] + jnp.dot(jnp.tanh(xwa), w2_ref[...], precision=HIGHEST, preferred_element_type=F32)
    w = -_softplus(-w) - 0.5
    lw_out[...] = -jnp.exp(w)
    a = jax.nn.sigmoid(a0_ref[...] + jnp.dot(xwa, a2_ref[...], precision=HIGHEST, preferred_element_type=F32))
    a_out[...] = a.astype(a_out.dtype)
    g_out[...] = jnp.dot(jax.nn.sigmoid(xg).astype(BF16), g2_ref[...],
                         preferred_element_type=F32).astype(g_out.dtype)
    r_out[...] = xr.astype(r_out.dtype)
    v_out[...] = jnp.where(valid, xv, 0.0).astype(v_out.dtype)
    k_out[...] = jnp.where(valid, xk * (1.0 + (a - 1.0) * ka_ref[...]), 0.0).astype(k_out.dtype)
    kk_out[...] = jnp.where(valid, xk * kk_ref[...], 0.0).astype(kk_out.dtype)


def _rwkv_pre(z, mu, w0, w2, a0, a2, g2, k_k, k_a, *, nb, tp, pad, tm):
    m = z.shape[0]
    mu_r, mu_k, mu_v = (mu[s:s + RW_WIDTH].reshape(1, RW_WIDTH) for s in (0, RW_WIDTH, 2 * RW_WIDTH))
    o = 3 * RW_WIDTH
    mu_wa = mu[o:o + 128].reshape(1, 128)
    mu_g = jnp.pad(mu[o + 128:o + 128 + RW_GATE_LORA], (0, 256 - RW_GATE_LORA)).reshape(1, 256)
    w2p = jnp.concatenate([w2, jnp.zeros_like(a2)], axis=0)
    a2p = jnp.concatenate([jnp.zeros_like(w2), a2], axis=0)
    g2p = jnp.pad(g2, ((0, 256 - RW_GATE_LORA), (0, 0)))
    kern = functools.partial(_rwkv_pre_kernel, tm=tm, tp=tp, pad=pad, nb=nb)

    def zspec(off, width):
        return pl.BlockSpec((tm, width), lambda i: (i, off // width))

    def full(shape):
        return pl.BlockSpec(shape, lambda i: (0,) * len(shape))

    vec = full((1, RW_WIDTH))
    wide = pl.BlockSpec((tm, RW_WIDTH), lambda i: (i, 0))
    outs = pl.pallas_call(
        kern,
        grid=(m // tm,),
        in_specs=[zspec(Z_R, RW_WIDTH), zspec(Z_K, RW_WIDTH), zspec(Z_V, RW_WIDTH), zspec(Z_GL, 256), zspec(Z_WA, 128),
                  vec, vec, vec, full((1, 256)), full((1, 128)),
                  vec, full((128, RW_WIDTH)), vec, full((128, RW_WIDTH)), full((256, RW_WIDTH)), vec, vec],
        out_specs=[wide] * 7,
        out_shape=[jax.ShapeDtypeStruct((m, RW_WIDTH), F32 if n == 1 else BF16) for n in range(7)],
        scratch_shapes=[pltpu.VMEM((tm + SUBLANES, RW_WIDTH), F32)] * 3
        + [pltpu.VMEM((tm + SUBLANES, 256), F32), pltpu.VMEM((tm + SUBLANES, 128), F32)],
        compiler_params=_params(("arbitrary",)),
        name="rwkv_pre",
    )(z, z, z, z, z, mu_r, mu_k, mu_v, mu_g, mu_wa,
      w0.reshape(1, -1), w2p, a0.reshape(1, -1), a2p, g2p.astype(BF16), k_k.reshape(1, -1), k_a.reshape(1, -1))
    return outs


_NN = (((1,), (0,)), ((), ()))
_NT = (((1,), (1,)), ((), ()))


def _split2(a):
    hi = a.astype(BF16)
    return hi, (a - hi.astype(F32)).astype(BF16)


def _dot(a, b, dims, passes):
    if passes == 6:
        return lax.dot_general(a, b, dims, precision=HIGHEST, preferred_element_type=F32)
    if passes == 1:
        return lax.dot_general(a.astype(BF16), b.astype(BF16), dims, preferred_element_type=F32)
    ah, al = _split2(a)
    bh, bl = _split2(b)
    out = lax.dot_general(ah, bh, dims, preferred_element_type=F32)
    out = out + lax.dot_general(ah, bl, dims, preferred_element_type=F32)
    return out + lax.dot_general(al, bh, dims, preferred_element_type=F32)


def _mm(a, b, passes=6):
    return _dot(a, b, _NN, passes)


def _mm_nt(a, b, passes=6):
    return _dot(a, b, _NT, passes)


def _bf(x):
    return x.astype(BF16)


def _bmm(a, b):
    return lax.dot_general(a, b, _NN, preferred_element_type=F32)


def _bmm_nt(a, b):
    return lax.dot_general(a, b, _NT, preferred_element_type=F32)


def _geometric_sums(xs, eye, steps):
    n = xs[0].shape[0]
    s = [eye + x for x in xs]
    if steps == 1:
        return s
    p = [_bmm(b, b) for b in map(_bf, xs)]
    yield
    for k in range(1, steps):
        pb = [_bf(pp) for pp in p]
        if k == steps - 1:
            s = [ss + _bmm(b, _bf(ss)) for ss, b in zip(s, pb)]
        else:
            both = [_bmm(b, jnp.concatenate([b, _bf(ss)], axis=1)) for b, ss in zip(pb, s)]
            p = [r[:, 0:n] for r in both]
            s = [ss + r[:, n:2 * n] for ss, r in zip(s, both)]
        yield
    return s


def _unit_lower_inverse(n_mats, blk):
    n = n_mats[0].shape[0]
    ri = lax.broadcasted_iota(jnp.int32, (n, n), 0)
    ci = lax.broadcasted_iota(jnp.int32, (n, n), 1)
    eye = (ri == ci).astype(F32)
    diag_blk = _block_id(ri, blk) == _block_id(ci, blk)
    d_inv = yield from _geometric_sums([jnp.where(diag_blk, -m, 0.0) for m in n_mats], eye, int(np.log2(blk)))
    d_inv_b = [_bf(d) for d in d_inv]
    m2 = [-_bmm(d, _bf(jnp.where(diag_blk, 0.0, m))) for d, m in zip(d_inv_b, n_mats)]
    yield
    t = yield from _geometric_sums(m2, eye, int(np.log2(RW_CHUNK // blk)))
    t_inv = [_bmm(_bf(tt), d) for tt, d in zip(t, d_inv_b)]
    yield
    return t_inv


def _rwkv_stages(r_ref, lw_ref, k_ref, v_ref, kk_ref, a_ref, g_ref, rk_ref, lng_ref, lnb_ref,
                 o_ref, st_ref, *, chunk, npair):
    @pl.when(pl.program_id(1) == 0)
    def _():
        st_ref[...] = jnp.zeros_like(st_ref)

    L = chunk
    n2 = 2 * L
    lane = lax.broadcasted_iota(jnp.int32, (1, LANES), 1)
    m0 = (lane < RW_HEAD).astype(F32)
    m1 = 1.0 - m0
    tri = (lax.broadcasted_iota(jnp.int32, (L, L), 0) >= lax.broadcasted_iota(jnp.int32, (L, L), 1)).astype(F32)
    ri = lax.broadcasted_iota(jnp.int32, (n2, n2), 0)
    ci = lax.broadcasted_iota(jnp.int32, (n2, n2), 1)
    strict = ri > ci
    incl = ri >= ci
    cum_all = _mm(tri, lw_ref[...])

    def stack(x):
        return jnp.concatenate([x * m0, x * m1], axis=0)

    def head_mean(x):
        s0 = jnp.sum(x * m0, axis=-1, keepdims=True)
        s1 = jnp.sum(x * m1, axis=-1, keepdims=True)
        return (s0 * m0 + s1 * m1) * (1.0 / RW_HEAD)

    pairs = range(npair)
    sls = [slice(p * LANES, (p + 1) * LANES) for p in pairs]
    ops = []
    for sl in sls:
        r = r_ref[:, sl].astype(F32)
        lw = lw_ref[:, sl]
        k = k_ref[:, sl].astype(F32)
        kk = kk_ref[:, sl].astype(F32)
        kn = kk / jnp.maximum(jnp.sqrt(head_mean(kk * kk) * RW_HEAD + 1e-24), 1e-12)
        cum = cum_all[:, sl]
        c_last = cum[L - 1:L, :]
        e_neg = jnp.exp(-cum)
        e_end = jnp.exp(c_last - cum)
        beta = kn * a_ref[:, sl].astype(F32)
        v_s = stack(v_ref[:, sl].astype(F32))
        ops.append(dict(
            al_t=_bf(stack(kn * jnp.exp(cum - lw))),
            r_t=_bf(stack(r * jnp.exp(cum))),
            be_h=_bf(stack(beta * e_neg)),
            k_h=_bf(stack(k * e_neg)),
            be_e=_bf(stack(beta * e_end)),
            k_e=_bf(stack(k * e_end)),
            v_s=v_s, v_b=_bf(v_s),
            w_end=jnp.exp(c_last)))
    yield

    sc = [_bmm_nt(jnp.concatenate([o["al_t"], o["r_t"]], axis=0), jnp.concatenate([o["be_h"], o["k_h"]], axis=0))
          for o in ops]
    a_ab = [jnp.where(strict, s[0:n2, 0:n2], 0.0) for s in sc]
    a_ak = [_bf(jnp.where(strict, s[0:n2, n2:2 * n2], 0.0)) for s in sc]
    a_rb = [_bf(jnp.where(incl, s[n2:2 * n2, 0:n2], 0.0)) for s in sc]
    a_rk = [_bf(jnp.where(incl, s[n2:2 * n2, n2:2 * n2], 0.0)) for s in sc]
    gy = [_bmm(jnp.concatenate([ak, rk], axis=0), o["v_b"]) for ak, rk, o in zip(a_ak, a_rk, ops)]
    yield
    t_inv = yield from _unit_lower_inverse(a_ab, 16)
    hats = [_bmm(_bf(t), jnp.concatenate([o["al_t"], _bf(g[0:n2])], axis=1)) for t, o, g in zip(t_inv, ops, gy)]
    yield

    st = [st_ref[p] for p in pairs]
    from_st = [_bmm_nt(jnp.concatenate([_bf(h[:, 0:LANES]), o["r_t"]], axis=0), _bf(s))
               for h, o, s in zip(hats, ops, st)]
    u = [-(f[0:n2] + h[:, LANES:2 * LANES]) for f, h in zip(from_st, hats)]
    yield
    y_u = [_bmm(a, _bf(uu)) for a, uu in zip(a_rb, u)]
    upd = [_bmm(_bf(jnp.concatenate([uu, o["v_s"]], axis=0).T), jnp.concatenate([o["be_e"], o["k_e"]], axis=0))
           for uu, o in zip(u, ops)]
    for p in pairs:
        st_ref[p] = st[p] * ops[p]["w_end"] + upd[p]
    yield
    y_st = [f[n2:2 * n2] for f in from_st]
    y_loc = [g[n2:2 * n2] for g in gy]

    for p, sl in enumerate(sls):
        y = y_st[p] + y_u[p] + y_loc[p]
        y = y[0:L] + y[L:n2]
        r = r_ref[:, sl].astype(F32)
        k = k_ref[:, sl].astype(F32)
        v = v_ref[:, sl].astype(F32)
        mu = head_mean(y)
        yc = y - mu
        var = head_mean(yc * yc)
        yn = yc * lax.rsqrt(var + RW_GN_EPS) * lng_ref[:, sl] + lnb_ref[:, sl]
        bonus = head_mean(r * k * rk_ref[:, sl]) * RW_HEAD
        o_ref[:, sl] = ((yn + bonus * v) * g_ref[:, sl]).astype(o_ref.dtype)


def _recurrences_kernel(hq_ref, hf_ref, hi_ref, hg_ref, lb_ref, ng_ref,
                        r_ref, lw_ref, k_ref, v_ref, kk_ref, a_ref, g_ref, rk_ref, lng_ref, lnb_ref,
                        ya_ref, yb_ref, hst_ref, rst_ref, *, pad):
    stages = [_rwkv_stages(r_ref, lw_ref, k_ref, v_ref, kk_ref, a_ref, g_ref, rk_ref, lng_ref, lnb_ref,
                           yb_ref, rst_ref, chunk=RW_CHUNK, npair=RW_WIDTH // LANES),
              _hgrn_stages(hq_ref, hf_ref, hi_ref, hg_ref, lb_ref, ng_ref, ya_ref, hst_ref,
                           chunk=HG_CHUNK, pad=pad, nheads=HG_HEADS)]
    while stages:
        for gen in list(stages):
            if next(gen, stages) is stages:
                stages.remove(gen)


def _recurrences(z, lb, norm_g, pre, r_k, ln_g, ln_b, *, nb, tp, pad):
    assert HG_CHUNK == RW_CHUNK and HG_WIDTH == RW_WIDTH
    m = z.shape[0]
    L = HG_CHUNK
    nc = tp // L

    def zspec(off):
        return pl.BlockSpec((L, HG_WIDTH), lambda b, c: (b * nc + c, off // HG_WIDTH))

    blk = pl.BlockSpec((L, RW_WIDTH), lambda b, c: (b * nc + c, 0))
    pspec = pl.BlockSpec((1, RW_WIDTH), lambda b, c: (0, 0))
    row = lambda x: x.reshape(1, -1).astype(F32)
    return pl.pallas_call(
        functools.partial(_recurrences_kernel, pad=pad),
        grid=(nb, nc),
        in_specs=[zspec(Z_HQ), zspec(Z_HF), zspec(Z_HI), zspec(Z_HG), pspec, pspec] + [blk] * 7 + [pspec] * 3,
        out_specs=[blk, blk],
        out_shape=[jax.ShapeDtypeStruct((m, HG_WIDTH), BF16), jax.ShapeDtypeStruct((m, RW_WIDTH), BF16)],
        scratch_shapes=[pltpu.VMEM((HG_HEADS, HG_DIM, HG_DIM), F32),
                        pltpu.VMEM((RW_WIDTH // LANES, LANES, LANES), F32)],
        compiler_params=_params(("arbitrary", "arbitrary")),
        name="recurrences",
    )(z, z, z, z, row(lb), row(norm_g), *pre, row(r_k), row(ln_g), row(ln_b))


def _rope_kernel(x_ref, cos_ref, sin_ref, o_ref, *, n_plain, n_rope, scale, is_query, tr, nrb, pad):
    lane = lax.broadcasted_iota(jnp.int32, (1, LANES), 1)
    cos = cos_ref[...]
    sin = sin_ref[...]
    if is_query:
        bias = jnp.ones((tr, 1), F32)
    else:
        pos = (pl.program_id(0) % nrb) * tr + lax.broadcasted_iota(jnp.int32, (tr, 1), 0)
        bias = jnp.where(pos < pad, NEG_BIG, 0.0)
    if n_plain:
        o_ref[:, 0:n_plain * LANES] = (x_ref[:, 0:n_plain * LANES] * scale).astype(o_ref.dtype)
    for h in range(n_plain, n_plain + n_rope):
        x = x_ref[:, h * LANES:(h + 1) * LANES]
        rot = jnp.where(lane < MLA_ROPE // 2, -pltpu.roll(x, LANES - MLA_ROPE // 2, 1),
                        pltpu.roll(x, MLA_ROPE // 2, 1))
        y = (x * cos + rot * sin) * scale
        o_ref[:, h * LANES:(h + 1) * LANES] = jnp.where(lane == MLA_ROPE, bias, y).astype(o_ref.dtype)


def _rope(x, col_block, n_plain, n_rope, cos, sin, scale, is_query, *, tp, tr, pad):
    m = x.shape[0]
    nrb = tp // tr
    width = (n_plain + n_rope) * LANES
    kern = functools.partial(_rope_kernel, n_plain=n_plain, n_rope=n_rope, scale=scale, is_query=is_query,
                             tr=tr, nrb=nrb, pad=pad)
    tab = pl.BlockSpec((tr, LANES), lambda i: (i % nrb, 0))
    return pl.pallas_call(
        kern,
        grid=(m // tr,),
        in_specs=[pl.BlockSpec((tr, width), lambda i: (i, col_block)), tab, tab],
        out_specs=pl.BlockSpec((tr, width), lambda i: (i, 0)),
        out_shape=jax.ShapeDtypeStruct((m, width), BF16),
        compiler_params=_params(("arbitrary",)),
        name="rope",
    )(x, cos, sin)


def _q_proj_kernel(x_ref, g_ref, w_ref, cos_ref, sin_ref, o_ref, xn_ref, *, n_plain, scale, eps):
    j = pl.program_id(1)

    @pl.when(j == 0)
    def _():
        xn_ref[...] = _normed_rows(x_ref[...], g_ref[...], eps, 0, 0, 0, 0).astype(BF16)

    y = jnp.dot(xn_ref[...], w_ref[...], preferred_element_type=F32)

    @pl.when(j < n_plain)
    def _():
        o_ref[...] = (y * scale).astype(o_ref.dtype)

    @pl.when(j >= n_plain)
    def _():
        lane = lax.broadcasted_iota(jnp.int32, (1, LANES), 1)
        cos = cos_ref[...]
        sin = sin_ref[...]
        for b in range(y.shape[1] // LANES):
            x = y[:, b * LANES:(b + 1) * LANES]
            rot = jnp.where(lane < MLA_ROPE // 2, -pltpu.roll(x, LANES - MLA_ROPE // 2, 1),
                            pltpu.roll(x, MLA_ROPE // 2, 1))
            r = (x * cos + rot * sin) * scale
            o_ref[:, b * LANES:(b + 1) * LANES] = jnp.where(lane == MLA_ROPE, 1.0, r).astype(o_ref.dtype)


def _q_proj(z, g, w, cos_rows, sin_rows, scale, *, tm, tn):
    m = z.shape[0]
    k = MLA_Q_RANK
    n = w.shape[1]
    kern = functools.partial(_q_proj_kernel, n_plain=MLA_HEADS * MLA_NOPE // tn, scale=scale, eps=EPS)
    tab = pl.BlockSpec((tm, LANES), lambda i, j: (i, 0))
    return pl.pallas_call(
        kern,
        grid=(m // tm, n // tn),
        in_specs=[pl.BlockSpec((tm, k), lambda i, j: (i, Z_CQ // k)),
                  pl.BlockSpec((1, k), lambda i, j: (0, 0)),
                  pl.BlockSpec((k, tn), lambda i, j: (0, j)), tab, tab],
        out_specs=pl.BlockSpec((tm, tn), lambda i, j: (i, j)),
        out_shape=jax.ShapeDtypeStruct((m, n), BF16),
        scratch_shapes=[pltpu.VMEM((tm, k), BF16)],
        compiler_params=_params(("arbitrary", "arbitrary")),
        name="q_proj",
    )(z, g.reshape(1, k).astype(F32), w, cos_rows, sin_rows)


def _attn_kernel(qn_ref, qr_ref, qn_next_ref, qr_next_ref, kn_ref, kr_ref, vt_ref, bias_ref, o_ref,
                 sg_ref, s0_ref, s1_ref, m_ref, l_ref, acc_ref, *, t, nq):
    qi = pl.program_id(2)
    q = jnp.concatenate([qn_ref[...], qr_ref[...]], axis=1)

    def scores_t(off):
        kcat = jnp.concatenate([kn_ref[pl.ds(off, 4 * t), :], kr_ref[pl.ds(off, 4 * t), :]], axis=1)
        return lax.dot_general(kcat, q, _NT, preferred_element_type=F32)

    def first_tile(jq):
        return jnp.minimum(4 * jq, nq - 4)

    def issue_scores(jq, dst_ref):
        first = first_tile(jq)
        s = scores_t(pl.multiple_of(first * t, t))
        for n in range(4):
            x = first + n
            kind = jnp.where(x < 4 * jq, 2, jnp.where(x < qi, 0, jnp.where(x == qi, 1, 2)))
            dst_ref[n * t:(n + 1) * t, :] = s[n * t:(n + 1) * t] + bias_ref[kind]

    def consume(jq, src_ref):
        first = first_tile(jq)
        for stream in range(2):
            s = src_ref[2 * stream * t:2 * (stream + 1) * t, :]
            m_i = m_ref[stream]
            m_new = jnp.maximum(m_i, jnp.max(s, axis=0, keepdims=True))
            alpha = jnp.exp2(m_i - m_new)
            p = jnp.exp2(s - m_new)
            m_ref[stream] = m_new
            l_ref[stream] = alpha * l_ref[stream] + jnp.sum(p, axis=0, keepdims=True)
            v_t = jnp.concatenate([vt_ref[0, first + 2 * stream], vt_ref[0, first + 2 * stream + 1]], axis=1)
            acc_ref[stream] = alpha * acc_ref[stream] + jnp.dot(v_t, p.astype(BF16), preferred_element_type=F32)

    def half(jq, cur_ref, nxt_ref):
        issue_scores(jq + 1, nxt_ref)
        consume(jq, cur_ref)

    def issue_next_tile():
        q_next = jnp.concatenate([qn_next_ref[...], qr_next_ref[...]], axis=1)
        kcat = jnp.concatenate([kn_ref[0:4 * t, :], kr_ref[0:4 * t, :]], axis=1)
        sg_ref[...] = lax.dot_general(kcat, q_next, _NT, preferred_element_type=F32)

    last = qi // 4
    m_ref[...] = jnp.full(m_ref.shape, NEG_BIG, F32)
    l_ref[...] = jnp.zeros(l_ref.shape, F32)
    acc_ref[...] = jnp.zeros(acc_ref.shape, F32)

    @pl.when(qi <= 4)
    def _():
        issue_scores(0, sg_ref)

    @pl.when(last == 0)
    def _():
        consume(0, sg_ref)

    @pl.when(last > 0)
    def _():
        issue_scores(1, s0_ref)
        consume(0, sg_ref)

        def trip(r, carry):
            half(1 + 2 * r, s0_ref, s1_ref)

            @pl.when(2 + 2 * r < last)
            def _():
                half(2 + 2 * r, s1_ref, s0_ref)

            return carry

        lax.fori_loop(0, last // 2, trip, 0)

        @pl.when(last % 2 == 1)
        def _():
            issue_next_tile()
            consume(last, s0_ref)

        @pl.when(last % 2 == 0)
        def _():
            issue_next_tile()
            consume(last, s1_ref)

    m_a, m_b = m_ref[0], m_ref[1]
    m_i = jnp.maximum(m_a, m_b)
    w_a = jnp.exp2(m_a - m_i)
    w_b = jnp.exp2(m_b - m_i)
    out_t = (w_a * acc_ref[0] + w_b * acc_ref[1]) / (w_a * l_ref[0] + w_b * l_ref[1])
    o_ref[...] = out_t.T.astype(o_ref.dtype)


def _attention(q, kr, kv, *, nb, tp):
    m = q.shape[0]
    tq = _pick(tp, tuple(c for c in (384, 256, 128) if tp >= 4 * c))
    nq = tp // tq
    kern = functools.partial(_attn_kernel, t=tq, nq=nq)
    allowed_t = jnp.arange(tq)[:, None] <= jnp.arange(tq)[None, :]
    bias = jnp.stack([jnp.zeros((tq, tq), F32), jnp.where(allowed_t, 0.0, NEG_BIG).astype(F32),
                      jnp.full((tq, tq), NEG_BIG, F32)])
    v_t = kv[:, MLA_HEADS * MLA_NOPE:].reshape(nb, nq, tq, MLA_HEADS, MLA_V)
    v_t = v_t.transpose(0, 3, 1, 4, 2).reshape(nb * MLA_HEADS, nq, MLA_V, tq)
    return pl.pallas_call(
        kern,
        grid=(nb, MLA_HEADS, nq),
        in_specs=[pl.BlockSpec((tq, LANES), lambda b, h, i: (b * nq + i, h)),
                  pl.BlockSpec((tq, LANES), lambda b, h, i: (b * nq + i, MLA_HEADS + h)),
                  pl.BlockSpec((tq, LANES), lambda b, h, i: (b * nq + jnp.minimum(i + 1, nq - 1), h)),
                  pl.BlockSpec((tq, LANES), lambda b, h, i: (b * nq + jnp.minimum(i + 1, nq - 1), MLA_HEADS + h)),
                  pl.BlockSpec((tp, LANES), lambda b, h, i: (b, h)),
                  pl.BlockSpec((tp, LANES), lambda b, h, i: (b, 0)),
                  pl.BlockSpec((1, nq, MLA_V, tq), lambda b, h, i: (b * MLA_HEADS + h, 0, 0, 0)),
                  pl.BlockSpec((3, tq, tq), lambda b, h, i: (0, 0, 0))],
        out_specs=pl.BlockSpec((tq, LANES), lambda b, h, i: (b * nq + i, h)),
        out_shape=jax.ShapeDtypeStruct((m, MLA_WIDTH), BF16),
        scratch_shapes=[pltpu.VMEM((4 * tq, tq), F32)] * 3
        + [pltpu.VMEM((2, 1, tq), F32), pltpu.VMEM((2, 1, tq), F32), pltpu.VMEM((2, MLA_V, tq), F32)],
        compiler_params=_params(("arbitrary", "arbitrary", "arbitrary")),
        name="mla_attention",
    )(q, q, q, q, kv, kr, v_t, bias)


def _merge_kernel(ya_ref, yb_ref, yc_ref, ga_ref, gb_ref, gc_ref, w_ref, o_ref):
    acc = None
    for n, (y_ref, g_ref) in enumerate(((ya_ref, ga_ref), (yb_ref, gb_ref), (yc_ref, gc_ref))):
        p = jnp.dot(y_ref[...], w_ref[n], preferred_element_type=F32)
        t = jax.nn.sigmoid(g_ref[...]) * p
        acc = t if acc is None else acc + t
    o_ref[...] = acc.astype(o_ref.dtype)


def _merge(ya, yb, yc, z, w_branch, *, tm, tn):
    m = ya.shape[0]
    d = w_branch.shape[2]
    yspec = pl.BlockSpec((tm, HG_WIDTH), lambda i, j: (i, 0))

    def gspec(n):
        return pl.BlockSpec((tm, tn), lambda i, j: (i, (Z_GATES + n * d) // tn + j))

    return pl.pallas_call(
        _merge_kernel,
        grid=(m // tm, d // tn),
        in_specs=[yspec, yspec, yspec, gspec(0), gspec(1), gspec(2),
                  pl.BlockSpec((N_BRANCH, HG_WIDTH, tn), lambda i, j: (0, 0, j))],
        out_specs=pl.BlockSpec((tm, tn), lambda i, j: (i, j)),
        out_shape=jax.ShapeDtypeStruct((m, d), BF16),
        compiler_params=_params(("arbitrary", "arbitrary")),
        name="merge",
    )(ya, yb, yc, z, z, z, w_branch)


def _matmul_res_kernel(x_ref, w_ref, r_ref, o_ref):
    o_ref[...] = r_ref[...] + jnp.dot(x_ref[...], w_ref[...], preferred_element_type=F32)


def _matmul_res(x, w, res, *, tm, tn):
    m, k = x.shape
    n = w.shape[1]
    return pl.pallas_call(
        _matmul_res_kernel,
        grid=(m // tm, n // tn),
        in_specs=[pl.BlockSpec((tm, k), lambda i, j: (i, 0)),
                  pl.BlockSpec((k, tn), lambda i, j: (0, j)),
                  pl.BlockSpec((tm, tn), lambda i, j: (i, j))],
        out_specs=pl.BlockSpec((tm, tn), lambda i, j: (i, j)),
        out_shape=jax.ShapeDtypeStruct((m, n), F32),
        compiler_params=_params(("arbitrary", "arbitrary")),
        name="matmul_res",
    )(x, w, res)


def _ffn_up_kernel(x_ref, g_ref, wg_ref, wv_ref, cwg_ref, cwv_ref, cbg_ref, cbv_ref, o_ref,
                   xn_ref, eg_ref, ev_ref, pg_ref, pv_ref, *, tm, tp, pad, nb, eps):
    i = pl.program_id(0)
    j = pl.program_id(1)

    @pl.when(j == 0)
    def _():
        xn_ref[...] = _normed_rows(x_ref[...], g_ref[...], eps, i * tm, tp, pad, nb).astype(BF16)

    @pl.when(i == 0)
    def _():
        pg_ref[j] = jnp.zeros(pg_ref.shape[1:], F32)
        pv_ref[j] = jnp.zeros(pv_ref.shape[1:], F32)

    def conv(w_ref, e_ref, p_ref, cw_ref, cb_ref):
        u = jnp.dot(xn_ref[...], w_ref[...], preferred_element_type=F32)
        e_ref[0:SUBLANES, :] = p_ref[j]
        e_ref[SUBLANES:SUBLANES + tm, :] = u
        p_ref[j] = u[tm - SUBLANES:tm, :]
        u1 = e_ref[SUBLANES - 1:SUBLANES - 1 + tm, :]
        u2 = e_ref[SUBLANES - 2:SUBLANES - 2 + tm, :]
        cw = cw_ref[...]
        return cw[0:1, :] * u2 + cw[1:2, :] * u1 + cw[2:3, :] * u + cb_ref[...]

    gate = conv(wg_ref, eg_ref, pg_ref, cwg_ref, cbg_ref)
    val = conv(wv_ref, ev_ref, pv_ref, cwv_ref, cbv_ref)
    o_ref[...] = (gate * jax.nn.sigmoid(gate) * val).astype(o_ref.dtype)


def _ffn_up(h, g, w_up, conv_w, conv_b, *, tp, pad, nb, tm, tf):
    m, d = h.shape
    dff = w_up.shape[1] // 2
    nj = dff // tf
    kern = functools.partial(_ffn_up_kernel, tm=tm, tp=tp, pad=pad, nb=nb, eps=EPS)
    cb = conv_b.reshape(1, -1).astype(F32)
    return pl.pallas_call(
        kern,
        grid=(m // tm, nj),
        in_specs=[pl.BlockSpec((tm, d), lambda i, j: (i, 0)),
                  pl.BlockSpec((1, d), lambda i, j: (0, 0)),
                  pl.BlockSpec((d, tf), lambda i, j: (0, j)),
                  pl.BlockSpec((d, tf), lambda i, j: (0, nj + j)),
                  pl.BlockSpec((CONV_W, tf), lambda i, j: (0, j)),
                  pl.BlockSpec((CONV_W, tf), lambda i, j: (0, nj + j)),
                  pl.BlockSpec((1, tf), lambda i, j: (0, j)),
                  pl.BlockSpec((1, tf), lambda i, j: (0, nj + j))],
        out_specs=pl.BlockSpec((tm, tf), lambda i, j: (i, j)),
        out_shape=jax.ShapeDtypeStruct((m, dff), BF16),
        scratch_shapes=[pltpu.VMEM((tm, d), BF16),
                        pltpu.VMEM((tm + SUBLANES, tf), F32), pltpu.VMEM((tm + SUBLANES, tf), F32),
                        pltpu.VMEM((nj, SUBLANES, tf), F32), pltpu.VMEM((nj, SUBLANES, tf), F32)],
        compiler_params=_params(("arbitrary", "arbitrary")),
        name="ffn_up",
    )(h, g.reshape(1, d).astype(F32), w_up, w_up, conv_w.astype(F32), conv_w.astype(F32), cb, cb)


def _final_norm_kernel(x_ref, g_ref, o_ref):
    x = x_ref[0]
    o_ref[0] = x * lax.rsqrt(jnp.mean(x * x, axis=-1, keepdims=True) + EPS) * g_ref[...]


def _final_norm(h3, g, *, skip, seq):
    nb, tp, d = h3.shape
    tr = LANES
    return pl.pallas_call(
        _final_norm_kernel,
        grid=(nb, seq // tr),
        in_specs=[pl.BlockSpec((1, tr, d), lambda b, i: (b, skip // tr + i, 0)),
                  pl.BlockSpec((1, d), lambda b, i: (0, 0))],
        out_specs=pl.BlockSpec((1, tr, d), lambda b, i: (b, i, 0)),
        out_shape=jax.ShapeDtypeStruct((nb, seq, d), F32),
        compiler_params=_params(("arbitrary", "arbitrary")),
        name="final_norm",
    )(h3, g.reshape(1, d).astype(F32))


def _layout_w_in(w):
    d = w.shape[0]
    o = 0
    parts = {}
    for name, width in (("hg4", 4 * HG_WIDTH), ("rkv", 3 * RW_WIDTH), ("wl", RW_DECAY_LORA), ("al", RW_AAA_LORA),
                        ("gl", RW_GATE_LORA), ("cq", MLA_Q_RANK), ("ckv", MLA_KV_RANK), ("kr", MLA_ROPE),
                        ("gates", N_BRANCH * d)):
        parts[name] = w[:, o:o + width].astype(BF16)
        o += width
    zeros = lambda n: jnp.zeros((d, n), BF16)
    cols = [parts["hg4"], parts["rkv"], parts["cq"], parts["ckv"],
            parts["gl"], zeros(256 - RW_GATE_LORA), parts["wl"], parts["al"],
            parts["kr"], zeros(LANES - MLA_ROPE), parts["gates"]]
    return jnp.concatenate(cols, axis=1)


def _layout_w_uq(w):
    r = w.shape[0]
    w = w.reshape(r, MLA_HEADS, MLA_NOPE + MLA_ROPE)
    nope = w[:, :, :MLA_NOPE].reshape(r, MLA_HEADS * MLA_NOPE)
    rope = jnp.pad(w[:, :, MLA_NOPE:], ((0, 0), (0, 0), (0, LANES - MLA_ROPE))).reshape(r, MLA_HEADS * LANES)
    return jnp.concatenate([nope, rope], axis=1).astype(BF16)


def _layout_w_ukv(w):
    r = w.shape[0]
    w = w.reshape(r, MLA_HEADS, MLA_NOPE + MLA_V)
    return jnp.concatenate([w[:, :, :MLA_NOPE].reshape(r, -1), w[:, :, MLA_NOPE:].reshape(r, -1)], axis=1).astype(BF16)


def _rope_tables(tp, pad):
    inv = ROPE_BASE ** (-jnp.arange(0, MLA_ROPE, 2, dtype=F32) / MLA_ROPE)
    pos = jnp.arange(tp, dtype=F32) - float(pad)
    ang = pos[:, None] * inv[None, :]
    z = jnp.zeros((tp, LANES - MLA_ROPE), F32)
    cos = jnp.concatenate([jnp.cos(ang), jnp.cos(ang), z], axis=1)
    sin = jnp.concatenate([jnp.sin(ang), jnp.sin(ang), z], axis=1)
    return cos, sin


def kernel(x, meta_tokens, norm1_g, w_in, hg_lb_logits, hg_norm_g, rw_mu, rw_w0, rw_w2, rw_a0, rw_a2, rw_g2, rw_k_k, rw_k_a, rw_r_k, rw_ln_g, rw_ln_b, mla_q_norm_g, mla_w_uq, mla_kv_norm_g, mla_w_ukv, w_branch, w_out, norm2_g, ffn_w_up, ffn_conv_w, ffn_conv_b, ffn_w_down, final_norm_g):
    nb, seq, d = x.shape
    depth = w_in.shape[0]
    t = N_META + seq
    pad = (-N_META) % LANES
    tp = pad + t
    m = nb * tp
    tm = _pick(m, (768, 384, 256, 128))
    tm_wide = _pick(m, (1376, 768, 384, 256, 128))
    geo = dict(tp=tp, pad=pad, nb=nb)

    meta = jnp.broadcast_to(meta_tokens[None].astype(x.dtype), (nb, N_META, d))
    h = jnp.concatenate([jnp.zeros((nb, pad, d), x.dtype), meta, x], axis=1).reshape(m, d)
    cos, sin = _rope_tables(tp, pad)
    cos_rows, sin_rows = jnp.tile(cos, (nb, 1)), jnp.tile(sin, (nb, 1))
    p_lb = jax.nn.softmax(hg_lb_logits.astype(F32), axis=0)
    lower_bounds = jnp.cumsum(p_lb, axis=0) - p_lb[0]
    scale = float((MLA_NOPE + MLA_ROPE) ** -0.5 * np.log2(np.e))

    for l in range(depth):
        z = _norm_matmul(h, 0, d, norm1_g[l], _layout_w_in(w_in[l]), F32, tm=tm_wide, tn=512, **geo)
        pre = _rwkv_pre(z, rw_mu[l], rw_w0[l], rw_w2[l], rw_a0[l], rw_a2[l], rw_g2[l], rw_k_k[l], rw_k_a[l],
                        tm=_pick(m, (384, 256, 128)), **geo)
        y_a, y_b = _recurrences(z, lower_bounds[l], hg_norm_g[l], pre, rw_r_k[l], rw_ln_g[l], rw_ln_b[l], **geo)
        q_cat = _q_proj(z, mla_q_norm_g[l], _layout_w_uq(mla_w_uq[l]), cos_rows, sin_rows, scale, tm=tm_wide, tn=512)
        kv = _norm_matmul(z, Z_CKV // MLA_KV_RANK, MLA_KV_RANK, mla_kv_norm_g[l], _layout_w_ukv(mla_w_ukv[l]), BF16,
                          tm=tm_wide, tn=512, tp=tp, pad=0, nb=nb)
        tr = _pick(tp, (384, 256, 128))
        k_rope = _rope(z, Z_KR // LANES, 0, 1, cos, sin, 1.0, False, tp=tp, tr=tr, pad=pad)
        y_c = _attention(q_cat, k_rope, kv, nb=nb, tp=tp)
        merged = _merge(y_a, y_b, y_c, z, w_branch[l].astype(BF16), tm=tm, tn=512)
        h = _matmul_res(merged, w_out[l].astype(BF16), h, tm=tm_wide, tn=512)
        act = _ffn_up(h, norm2_g[l], ffn_w_up[l].astype(BF16), ffn_conv_w[l], ffn_conv_b[l], tm=tm, tf=512, **geo)
        h = _matmul_res(act, ffn_w_down[l].astype(BF16), h, tm=tm, tn=512)
    return _final_norm(h.reshape(nb, tp, d), final_norm_g, skip=pad + N_META, seq=seq)
```
